```python
import math
import jax, jax.numpy as jnp
from jax import lax
import numpy as np

D_MODEL = 2048
BATCH = 2
SEQ = 16384
DEPTH = 4

MIX_WIDTH = D_MODEL
ATTN_WIDTH = MIX_WIDTH // 2
CONV_WIDTH = MIX_WIDTH - ATTN_WIDTH
N_HEADS = 8
HEAD_DIM = ATTN_WIDTH // N_HEADS
QK_DIM = HEAD_DIM // 2
CONV_K = 3
D_FF = ((8 * D_MODEL + 3 * 256 - 1) // (3 * 256)) * 256
N_BUCKETS = 32
MAX_DISTANCE = 128
Q_BLOCK = 128
NORM_EPS = 1e-6
SUBLN_EPS = 1e-5
IN_COLS = 3 * ATTN_WIDTH + 3 * CONV_WIDTH

kernel_name = "hybrid_diffattn_shortconv_sandwich"


def rms_norm(x, g, eps=NORM_EPS):
    xf = x.astype(jnp.float32)
    y = xf * lax.rsqrt(jnp.mean(xf * xf, axis=-1, keepdims=True) + eps)
    return (y * g.astype(jnp.float32)).astype(x.dtype)


def rel_bucket(dist):
    n = jnp.maximum(dist, 0)
    max_exact = N_BUCKETS // 2
    nf = jnp.maximum(n, max_exact).astype(jnp.float32)
    large = max_exact + (jnp.log(nf / max_exact) / math.log(MAX_DISTANCE / max_exact)
                         * (N_BUCKETS - max_exact)).astype(jnp.int32)
    large = jnp.minimum(large, N_BUCKETS - 1)
    return jnp.where(n < max_exact, n, large)


def diff_attention(q, k, v, lam, rel_bias):
    b, s = q.shape[0], q.shape[1]
    nb = s // Q_BLOCK
    scale = QK_DIM ** -0.5
    k_pos = jnp.arange(s, dtype=jnp.int32)
    q_blocks = q.reshape(b, nb, Q_BLOCK, N_HEADS, 2, QK_DIM).transpose(1, 0, 2, 3, 4, 5)
    starts = jnp.arange(nb, dtype=jnp.int32) * Q_BLOCK

    def block(args):
        q_blk, start = args
        q_pos = start + jnp.arange(Q_BLOCK, dtype=jnp.int32)
        dist = q_pos[:, None] - k_pos[None, :]
        bias = jnp.transpose(rel_bias[rel_bucket(dist)], (2, 0, 1)).astype(jnp.float32)
        logits = jnp.einsum('bqhcd,bkhcd->bhcqk', q_blk, k).astype(jnp.float32) * scale
        logits = logits + bias[None, :, None]
        logits = jnp.where(dist >= 0, logits, -jnp.inf)
        p = jax.nn.softmax(logits, axis=-1)
        w = (p[:, :, 0] - lam * p[:, :, 1]).astype(v.dtype)
        return jnp.einsum('bhqk,bkhe->bqhe', w, v)

    o = lax.map(block, (q_blocks, starts))
    return o.transpose(1, 0, 2, 3, 4).reshape(b, s, N_HEADS, HEAD_DIM)


def short_conv(h, w):
    return lax.conv_general_dilated(
        h, w[:, None, :].astype(h.dtype), window_strides=(1,),
        padding=[(CONV_K - 1, 0)], dimension_numbers=('NWC', 'WIO', 'NWC'),
        feature_group_count=h.shape[-1])


def setup_inputs(seed: int = 0) -> dict:
    key = jax.random.key(seed)
    ks = jax.random.split(key, 20)
    f32 = jnp.float32
    nrm = lambda k, shape, s: jax.random.normal(k, shape, f32) * s
    gain = lambda k, shape: 1.0 + 0.05 * jax.random.normal(k, shape, f32)
    return {
        "x": jax.random.normal(ks[0], (BATCH, SEQ, D_MODEL), f32),
        "w_in": nrm(ks[1], (DEPTH, D_MODEL, IN_COLS), D_MODEL ** -0.5),
        "w_out": nrm(ks[2], (DEPTH, MIX_WIDTH, D_MODEL), MIX_WIDTH ** -0.5),
        "lambda_q1": nrm(ks[3], (DEPTH, QK_DIM), 0.1),
        "lambda_k1": nrm(ks[4], (DEPTH, QK_DIM), 0.1),
        "lambda_q2": nrm(ks[5], (DEPTH, QK_DIM), 0.1),
        "lambda_k2": nrm(ks[6], (DEPTH, QK_DIM), 0.1),
        "subln_gain": gain(ks[7], (DEPTH, HEAD_DIM)),
        "conv_w": nrm(ks[8], (DEPTH, CONV_K, CONV_WIDTH), CONV_K ** -0.5),
        "conv_norm_gain": gain(ks[9], (DEPTH, CONV_WIDTH)),
        "rel_bias": nrm(ks[10], (N_BUCKETS, N_HEADS), 0.5),
        "w_gate": nrm(ks[11], (DEPTH, D_MODEL, D_FF), D_MODEL ** -0.5),
        "w_up": nrm(ks[12], (DEPTH, D_MODEL, D_FF), D_MODEL ** -0.5),
        "w_down": nrm(ks[13], (DEPTH, D_FF, D_MODEL), D_FF ** -0.5),
        "norm_mix_pre": gain(ks[14], (DEPTH, D_MODEL)),
        "norm_mix_post": gain(ks[15], (DEPTH, D_MODEL)),
        "norm_ffn_pre": gain(ks[16], (DEPTH, D_MODEL)),
        "norm_ffn_post": gain(ks[17], (DEPTH, D_MODEL)),
    }


def reference(x, w_in, w_out, lambda_q1, lambda_k1, lambda_q2, lambda_k2, subln_gain,
              conv_w, conv_norm_gain, rel_bias, w_gate, w_up, w_down,
              norm_mix_pre, norm_mix_post, norm_ffn_pre, norm_ffn_post):
    b, s, _ = x.shape
    A, C = ATTN_WIDTH, CONV_WIDTH
    split_at = [A, 2 * A, 3 * A, 3 * A + C, 3 * A + 2 * C]
    for l in range(DEPTH):
        lambda_init = 0.8 - 0.6 * math.exp(-0.3 * l)
        hn = rms_norm(x, norm_mix_pre[l])
        proj = hn @ w_in[l]
        q, k, v, gate_b, gate_c, hc = jnp.split(proj, split_at, axis=-1)
        q = q.reshape(b, s, N_HEADS, 2, QK_DIM)
        k = k.reshape(b, s, N_HEADS, 2, QK_DIM)
        v = v.reshape(b, s, N_HEADS, HEAD_DIM)
        lam = (jnp.exp(jnp.sum(lambda_q1[l].astype(jnp.float32) * lambda_k1[l].astype(jnp.float32)))
               - jnp.exp(jnp.sum(lambda_q2[l].astype(jnp.float32) * lambda_k2[l].astype(jnp.float32)))
               + lambda_init)
        a = diff_attention(q, k, v, lam, rel_bias)
        a = rms_norm(a, subln_gain[l], SUBLN_EPS) * (1.0 - lambda_init)
        a = a.reshape(b, s, A)
        c = gate_b * short_conv(gate_c * hc, conv_w[l])
        c = rms_norm(c, conv_norm_gain[l])
        mix = jnp.concatenate([a, c], axis=-1) @ w_out[l]
        x = x + rms_norm(mix, norm_mix_post[l])
        hn = rms_norm(x, norm_ffn_pre[l])
        f = (jax.nn.silu(hn @ w_gate[l]) * (hn @ w_up[l])) @ w_down[l]
        x = x + rms_norm(f, norm_ffn_post[l])
    return x
```

```python
import functools
import math

import jax
import jax.numpy as jnp
from jax import lax
from jax.experimental import pallas as pl
from jax.experimental.pallas import tpu as pltpu

N_HEADS = 8
HEAD_DIM = 128
QK_DIM = 64
CONV_K = 3
N_BUCKETS = 32
MAX_DISTANCE = 128
NORM_EPS = 1e-6
SUBLN_EPS = 1e-5

LANES = 128
SUBLANES = 8
VMEM_LIMIT = 56 * 1024 * 1024

ATTN_TILE = 512
MASK_VALUE = -1e30

F32 = jnp.float32
BF16 = jnp.bfloat16


def _params(*sem):
    return pltpu.CompilerParams(dimension_semantics=sem, vmem_limit_bytes=VMEM_LIMIT)


def _rms(x, eps):
    return x * lax.rsqrt(jnp.mean(x * x, axis=-1, keepdims=True) + eps)


def _bias_kernel(rb_ref, o_ref, *, tile):
    h = pl.program_id(0)
    shape = (tile, 2 * tile)
    dist = tile + lax.broadcasted_iota(jnp.int32, shape, 0) - lax.broadcasted_iota(jnp.int32, shape, 1)
    n = jnp.maximum(dist, 0)
    max_exact = N_BUCKETS // 2
    nf = jnp.maximum(n, max_exact).astype(F32)
    large = max_exact + (jnp.log(nf / max_exact) / math.log(MAX_DISTANCE / max_exact)
                         * (N_BUCKETS - max_exact)).astype(jnp.int32)
    large = jnp.minimum(large, N_BUCKETS - 1)
    bucket = jnp.where(n < max_exact, n, large)
    val = jnp.zeros(shape, F32)
    for b in range(N_BUCKETS):
        val = jnp.where(bucket == b, rb_ref[b, h], val)
    val = val - rb_ref[N_BUCKETS - 1, h]
    o_ref[0] = jnp.where(dist >= 0, val, MASK_VALUE)


def _bias_tiles(rel_bias, tile):
    return pl.pallas_call(
        functools.partial(_bias_kernel, tile=tile),
        grid=(N_HEADS,),
        in_specs=[pl.BlockSpec(memory_space=pltpu.SMEM)],
        out_specs=pl.BlockSpec((1, tile, 2 * tile), lambda h: (h, 0, 0)),
        out_shape=jax.ShapeDtypeStruct((N_HEADS, tile, 2 * tile), F32),
        compiler_params=_params("arbitrary"),
        name="rel_bias_tiles",
    )(rel_bias)


def _norm_matmul_kernel(x_ref, g_ref, w_ref, o_ref, hn_ref):
    @pl.when(pl.program_id(1) == 0)
    def _():
        hn_ref[...] = (_rms(x_ref[...], NORM_EPS) * g_ref[...]).astype(BF16)

    o_ref[...] = jnp.dot(hn_ref[...], w_ref[...], preferred_element_type=F32).astype(o_ref.dtype)


def _norm_matmul(x, g, w, *, tm, tn):
    m, d = x.shape
    n = w.shape[1]
    return pl.pallas_call(
        _norm_matmul_kernel,
        grid=(m // tm, n // tn),
        in_specs=[
            pl.BlockSpec((tm, d), lambda i, j: (i, 0)),
            pl.BlockSpec((1, d), lambda i, j: (0, 0)),
            pl.BlockSpec((d, tn), lambda i, j: (0, j)),
        ],
        out_specs=pl.BlockSpec((tm, tn), lambda i, j: (i, j)),
        out_shape=jax.ShapeDtypeStruct((m, n), BF16),
        scratch_shapes=[pltpu.VMEM((tm, d), BF16)],
        compiler_params=_params("arbitrary", "arbitrary"),
        name="norm_in_proj",
    )(x, g, w)


def _attn_kernel(q_ref, k_ref, v_ref, bias_ref, lq1_ref, lk1_ref, lq2_ref, lk2_ref, g_ref,
                 o_ref, m_ref, l_ref, acc_ref, *, tile, lambda_init):
    qi = pl.program_id(2)
    reps = tile // LANES

    q = q_ref[0] * (QK_DIM ** -0.5)
    lane = lax.broadcasted_iota(jnp.int32, q.shape, 1)
    zero = jnp.zeros_like(q)
    q_halves = (jnp.where(lane < QK_DIM, q, zero), jnp.where(lane >= QK_DIM, q, zero))

    m_ref[...] = jnp.full(m_ref.shape, MASK_VALUE, F32)
    l_ref[...] = jnp.zeros(l_ref.shape, F32)
    acc_ref[...] = jnp.zeros(acc_ref.shape, F32)

    def step(kt, vt, bias):
        for c in range(2):
            s = lax.dot_general(q_halves[c], kt, (((1,), (1,)), ((), ())),
                                preferred_element_type=F32)
            if bias is not None:
                s = s + bias
            m_prev = m_ref[c]
            m_new = jnp.maximum(m_prev, jnp.max(s, axis=1, keepdims=True))
            alpha = jnp.exp(m_prev - m_new)
            p = jnp.exp(s - jnp.tile(m_new, (1, reps)))
            l_ref[c] = alpha * l_ref[c] + jnp.sum(p, axis=1, keepdims=True)
            acc_ref[c] = alpha * acc_ref[c] + jnp.dot(p.astype(BF16), vt,
                                                      preferred_element_type=F32)
            m_ref[c] = m_new

    def far_tile(j, carry):
        start = pl.multiple_of(j * tile, tile)
        step(k_ref[0, pl.ds(start, tile), :], v_ref[0, pl.ds(start, tile), :], None)
        return carry

    lax.fori_loop(0, jnp.maximum(qi - 1, 0), far_tile, 0)

    @pl.when(qi >= 1)
    def _():
        start = pl.multiple_of((qi - 1) * tile, tile)
        step(k_ref[0, pl.ds(start, tile), :], v_ref[0, pl.ds(start, tile), :],
             bias_ref[0, :, :tile])

    start = pl.multiple_of(qi * tile, tile)
    step(k_ref[0, pl.ds(start, tile), :], v_ref[0, pl.ds(start, tile), :],
         bias_ref[0, :, tile:])

    lam = (jnp.exp(jnp.sum(lq1_ref[...] * lk1_ref[...], axis=-1, keepdims=True))
           - jnp.exp(jnp.sum(lq2_ref[...] * lk2_ref[...], axis=-1, keepdims=True))
           + lambda_init)
    a = acc_ref[0] / l_ref[0] - lam * (acc_ref[1] / l_ref[1])
    a = _rms(a, SUBLN_EPS) * g_ref[...] * (1.0 - lambda_init)
    o_ref[0] = a.astype(o_ref.dtype)


def _attention(proj, bias_tab, lq1, lk1, lq2, lk2, gain, *, lambda_init):
    b, s, _ = proj.shape
    tile = ATTN_TILE
    vec = lambda n: pl.BlockSpec((1, n), lambda bi, h, qi: (0, 0))
    return pl.pallas_call(
        functools.partial(_attn_kernel, tile=tile, lambda_init=lambda_init),
        grid=(b, N_HEADS, s // tile),
        in_specs=[
            pl.BlockSpec((1, tile, HEAD_DIM), lambda bi, h, qi: (bi, qi, h)),
            pl.BlockSpec((1, s, HEAD_DIM), lambda bi, h, qi: (bi, 0, N_HEADS + h)),
            pl.BlockSpec((1, s, HEAD_DIM), lambda bi, h, qi: (bi, 0, 2 * N_HEADS + h)),
            pl.BlockSpec((1, tile, 2 * tile), lambda bi, h, qi: (h, 0, 0)),
            vec(QK_DIM), vec(QK_DIM), vec(QK_DIM), vec(QK_DIM), vec(HEAD_DIM),
        ],
        out_specs=pl.BlockSpec((1, tile, HEAD_DIM), lambda bi, h, qi: (bi, qi, h)),
        out_shape=jax.ShapeDtypeStruct((b, s, N_HEADS * HEAD_DIM), BF16),
        scratch_shapes=[
            pltpu.VMEM((2, tile, LANES), F32),
            pltpu.VMEM((2, tile, LANES), F32),
            pltpu.VMEM((2, tile, HEAD_DIM), F32),
        ],
        compiler_params=_params("arbitrary", "arbitrary", "arbitrary"),
        name="diff_attention",
    )(proj, proj, proj, bias_tab, lq1, lk1, lq2, lk2, gain)


def _conv_kernel(gb_ref, gc_ref, hc_ref, gc_halo_ref, hc_halo_ref, w_ref, g_ref, o_ref):
    i = pl.program_id(1)
    u = gc_ref[0].astype(F32) * hc_ref[0].astype(F32)
    halo = gc_halo_ref[0].astype(F32) * hc_halo_ref[0].astype(F32)
    halo = halo * (i > 0).astype(F32)
    row = lax.broadcasted_iota(jnp.int32, u.shape, 0)
    u1 = jnp.where(row == 0, halo[7:8], pltpu.roll(u, 1, 0))
    u2 = jnp.where(row == 0, halo[6:7], jnp.where(row == 1, halo[7:8], pltpu.roll(u, 2, 0)))
    w = w_ref[...]
    y = w[0:1] * u2 + w[1:2] * u1 + w[2:3] * u
    c = gb_ref[0].astype(F32) * y
    o_ref[0] = (_rms(c, NORM_EPS) * g_ref[...]).astype(o_ref.dtype)


def _short_conv(proj, conv_w, gain, *, tc):
    b, s, _ = proj.shape
    cw = conv_w.shape[1]
    nblk = 3 * (N_HEADS * HEAD_DIM) // cw
    halo_blocks = tc // SUBLANES
    col = lambda k: pl.BlockSpec((1, tc, cw), lambda bi, i: (bi, i, nblk + k))
    halo = lambda k: pl.BlockSpec(
        (1, SUBLANES, cw), lambda bi, i: (bi, jnp.maximum(i * halo_blocks - 1, 0), nblk + k))
    return pl.pallas_call(
        _conv_kernel,
        grid=(b, s // tc),
        in_specs=[
            col(0), col(1), col(2), halo(1), halo(2),
            pl.BlockSpec((CONV_K, cw), lambda bi, i: (0, 0)),
            pl.BlockSpec((1, cw), lambda bi, i: (0, 0)),
        ],
        out_specs=pl.BlockSpec((1, tc, cw), lambda bi, i: (bi, i, 0)),
        out_shape=jax.ShapeDtypeStruct((b, s, cw), BF16),
        compiler_params=_params("arbitrary", "arbitrary"),
        name="short_conv",
    )(proj, proj, proj, proj, proj, conv_w, gain)


def _out_proj_kernel(a_ref, c_ref, w_ref, x_ref, g_ref, o_ref):
    aw = a_ref.shape[1]
    mix = jnp.dot(a_ref[...], w_ref[:aw, :], preferred_element_type=F32)
    mix = mix + jnp.dot(c_ref[...], w_ref[aw:, :], preferred_element_type=F32)
    o_ref[...] = x_ref[...] + _rms(mix, NORM_EPS) * g_ref[...]


def _out_proj(a, c, w, x, g, *, tm):
    m, d = x.shape
    row = lambda n: pl.BlockSpec((tm, n), lambda i: (i, 0))
    return pl.pallas_call(
        _out_proj_kernel,
        grid=(m // tm,),
        in_specs=[
            row(a.shape[1]), row(c.shape[1]),
            pl.BlockSpec(w.shape, lambda i: (0, 0)),
            row(d),
            pl.BlockSpec((1, d), lambda i: (0, 0)),
        ],
        out_specs=row(d),
        out_shape=jax.ShapeDtypeStruct((m, d), F32),
        compiler_params=_params("arbitrary"),
        name="out_proj",
    )(a, c, w, x, g)


def _ffn_kernel(x_ref, gpre_ref, wg_ref, wu_ref, wd_ref, gpost_ref, o_ref, hn_ref):
    j = pl.program_id(1)

    @pl.when(j == 0)
    def _():
        hn_ref[...] = (_rms(x_ref[...], NORM_EPS) * gpre_ref[...]).astype(BF16)

    hn = hn_ref[...]
    gate = jnp.dot(hn, wg_ref[...], preferred_element_type=F32)
    up = jnp.dot(hn, wu_ref[...], preferred_element_type=F32)
    h = (gate * jax.nn.sigmoid(gate) * up).astype(BF16)
    part = jnp.dot(h, wd_ref[...], preferred_element_type=F32)

    @pl.when(j == 0)
    def _():
        o_ref[...] = part

    @pl.when(j > 0)
    def _():
        o_ref[...] += part

    @pl.when(j == pl.num_programs(1) - 1)
    def _():
        o_ref[...] = x_ref[...] + _rms(o_ref[...], NORM_EPS) * gpost_ref[...]


def _ffn(x, gpre, wg, wu, wd, gpost, *, tm, tf):
    m, d = x.shape
    f = wg.shape[1]
    return pl.pallas_call(
        _ffn_kernel,
        grid=(m // tm, f // tf),
        in_specs=[
            pl.BlockSpec((tm, d), lambda i, j: (i, 0)),
            pl.BlockSpec((1, d), lambda i, j: (0, 0)),
            pl.BlockSpec((d, tf), lambda i, j: (0, j)),
            pl.BlockSpec((d, tf), lambda i, j: (0, j)),
            pl.BlockSpec((tf, d), lambda i, j: (j, 0)),
            pl.BlockSpec((1, d), lambda i, j: (0, 0)),
        ],
        out_specs=pl.BlockSpec((tm, d), lambda i, j: (i, 0)),
        out_shape=jax.ShapeDtypeStruct((m, d), F32),
        scratch_shapes=[pltpu.VMEM((tm, d), BF16)],
        compiler_params=_params("arbitrary", "arbitrary"),
        name="swiglu_ffn",
    )(x, gpre, wg, wu, wd, gpost)


def kernel(x, w_in, w_out, lambda_q1, lambda_k1, lambda_q2, lambda_k2, subln_gain, conv_w, conv_norm_gain, rel_bias, w_gate, w_up, w_down, norm_mix_pre, norm_mix_post, norm_ffn_pre, norm_ffn_post):
    b, s, d = x.shape
    depth = w_in.shape[0]
    assert s % ATTN_TILE == 0 and ATTN_TILE >= MAX_DISTANCE

    w_in, w_out, w_gate, w_up, w_down = (w.astype(BF16) for w in (w_in, w_out, w_gate, w_up, w_down))
    bias_tab = _bias_tiles(rel_bias, ATTN_TILE)
    row = lambda v: v.reshape(1, -1)

    xf = x.reshape(b * s, d)
    for l in range(depth):
        lambda_init = 0.8 - 0.6 * math.exp(-0.3 * l)
        proj = _norm_matmul(xf, row(norm_mix_pre[l]), w_in[l], tm=1024, tn=1024)
        proj = proj.reshape(b, s, -1)
        a = _attention(proj, bias_tab, row(lambda_q1[l]), row(lambda_k1[l]), row(lambda_q2[l]),
                       row(lambda_k2[l]), row(subln_gain[l]), lambda_init=lambda_init)
        c = _short_conv(proj, conv_w[l], row(conv_norm_gain[l]), tc=512)
        xf = _out_proj(a.reshape(b * s, -1), c.reshape(b * s, -1), w_out[l], xf,
                       row(norm_mix_post[l]), tm=512)
        xf = _ffn(xf, row(norm_ffn_pre[l]), w_gate[l], w_up[l], w_down[l], row(norm_ffn_post[l]),
                  tm=512, tf=512)
    return xf.reshape(b, s, d)
```

```python
import functools
import math

import jax
import jax.numpy as jnp
from jax import lax
from jax.experimental import pallas as pl
from jax.experimental.pallas import tpu as pltpu

N_HEADS = 8
HEAD_DIM = 128
QK_DIM = 64
CONV_K = 3
N_BUCKETS = 32
MAX_DISTANCE = 128
NORM_EPS = 1e-6
SUBLN_EPS = 1e-5

LANES = 128
SUBLANES = 8
VMEM_LIMIT = 56 * 1024 * 1024

ATTN_TILE = 512
MASK_VALUE = -1e30
LOG2E = math.log2(math.e)

F32 = jnp.float32
BF16 = jnp.bfloat16


def _params(*sem):
    return pltpu.CompilerParams(dimension_semantics=sem, vmem_limit_bytes=VMEM_LIMIT)


def _rms(x, eps):
    return x * lax.rsqrt(jnp.mean(x * x, axis=-1, keepdims=True) + eps)


def _bias_kernel(rb_ref, o_ref, *, tile):
    h = pl.program_id(0)
    shape = (2 * tile, tile)
    dist = tile + lax.broadcasted_iota(jnp.int32, shape, 1) - lax.broadcasted_iota(jnp.int32, shape, 0)
    n = jnp.maximum(dist, 0)
    max_exact = N_BUCKETS // 2
    nf = jnp.maximum(n, max_exact).astype(F32)
    large = max_exact + (jnp.log(nf / max_exact) / math.log(MAX_DISTANCE / max_exact)
                         * (N_BUCKETS - max_exact)).astype(jnp.int32)
    large = jnp.minimum(large, N_BUCKETS - 1)
    bucket = jnp.where(n < max_exact, n, large)
    val = jnp.zeros(shape, F32)
    for b in range(N_BUCKETS):
        val = jnp.where(bucket == b, rb_ref[b, h], val)
    val = (val - rb_ref[N_BUCKETS - 1, h]) * LOG2E
    o_ref[0] = jnp.where(dist >= 0, val, MASK_VALUE)


def _bias_tiles(rel_bias, tile):
    return pl.pallas_call(
        functools.partial(_bias_kernel, tile=tile),
        grid=(N_HEADS,),
        in_specs=[pl.BlockSpec(memory_space=pltpu.SMEM)],
        out_specs=pl.BlockSpec((1, 2 * tile, tile), lambda h: (h, 0, 0)),
        out_shape=jax.ShapeDtypeStruct((N_HEADS, 2 * tile, tile), F32),
        compiler_params=_params("arbitrary"),
        name="rel_bias_tiles",
    )(rel_bias)


def _norm_matmul_kernel(x_ref, g_ref, w_ref, o_ref, vt_ref, hn_ref, *, v_block):
    j = pl.program_id(1)

    @pl.when(j == 0)
    def _():
        hn_ref[...] = (_rms(x_ref[...], NORM_EPS) * g_ref[...]).astype(BF16)

    r = jnp.dot(hn_ref[...], w_ref[...], preferred_element_type=F32)
    o_ref[...] = r.astype(o_ref.dtype)

    @pl.when(j == v_block)
    def _():
        vt_ref[...] = r.T.astype(vt_ref.dtype)


def _norm_matmul(x, g, w, *, tm, tn, v_block):
    m, d = x.shape
    n = w.shape[1]
    return pl.pallas_call(
        functools.partial(_norm_matmul_kernel, v_block=v_block),
        grid=(m // tm, n // tn),
        in_specs=[
            pl.BlockSpec((tm, d), lambda i, j: (i, 0)),
            pl.BlockSpec((1, d), lambda i, j: (0, 0)),
            pl.BlockSpec((d, tn), lambda i, j: (0, j)),
        ],
        out_specs=[
            pl.BlockSpec((tm, tn), lambda i, j: (i, j)),
            pl.BlockSpec((tn, tm), lambda i, j: (0, i)),
        ],
        out_shape=[
            jax.ShapeDtypeStruct((m, n), BF16),
            jax.ShapeDtypeStruct((tn, m), BF16),
        ],
        scratch_shapes=[pltpu.VMEM((tm, d), BF16)],
        compiler_params=_params("arbitrary", "arbitrary"),
        name="norm_in_proj",
    )(x, g, w)


def _attn_kernel(q_ref, k_ref, vt_ref, bias_ref, lq1_ref, lk1_ref, lq2_ref, lk2_ref, g_ref,
                 o_ref, s_ref, m_ref, alpha_ref, l_ref, acc_ref, *, tile, lambda_init):
    qi = pl.program_id(2)

    q = (q_ref[0].astype(F32) * (QK_DIM ** -0.5 * LOG2E)).astype(BF16)
    lane = lax.broadcasted_iota(jnp.int32, q.shape, 1)
    zero = jnp.zeros_like(q)
    q_halves = (jnp.where(lane < QK_DIM, q, zero), jnp.where(lane >= QK_DIM, q, zero))

    m_ref[...] = jnp.full(m_ref.shape, MASK_VALUE, F32)
    l_ref[...] = jnp.zeros(l_ref.shape, F32)
    acc_ref[...] = jnp.zeros(acc_ref.shape, F32)

    def qk(start):
        kt = k_ref[0, pl.ds(start, tile), :]
        return [lax.dot_general(kt, q_halves[c], (((1,), (1,)), ((), ())),
                                preferred_element_type=F32) for c in range(2)]

    def put_scores(scores, bias):
        for c, s in enumerate(scores):
            if bias is not None:
                s = s + bias
            m_prev = m_ref[c]
            m_new = jnp.maximum(m_prev, jnp.max(s, axis=0, keepdims=True))
            alpha_ref[c] = jnp.exp2(m_prev - m_new)
            m_ref[c] = m_new
            s_ref[c] = s

    def accumulate(start):
        vt = vt_ref[:, pl.ds(start, tile)]
        for c in range(2):
            alpha = alpha_ref[c]
            p = jnp.exp2(s_ref[c] - m_ref[c])
            l_ref[c] = alpha * l_ref[c] + jnp.sum(p, axis=0, keepdims=True)
            acc_ref[c] = alpha * acc_ref[c] + jnp.dot(vt, p.astype(BF16),
                                                      preferred_element_type=F32)

    diag = pl.multiple_of(qi * tile, tile)
    before = pl.multiple_of(jnp.maximum(qi - 1, 0) * tile, tile)
    put_scores(qk(diag), bias_ref[0, tile:, :])

    @pl.when(qi >= 1)
    def _():
        nxt = qk(before)
        accumulate(diag)
        put_scores(nxt, bias_ref[0, :tile, :])

    def far_tile(j, cur):
        nxt = qk(pl.multiple_of(j * tile, tile))
        accumulate(pl.multiple_of(cur * tile, tile))
        put_scores(nxt, None)
        return j

    last = lax.fori_loop(0, jnp.maximum(qi - 1, 0), far_tile, jnp.maximum(qi - 1, 0))
    accumulate(pl.multiple_of(last * tile, tile))

    lam =(jnp.exp(jnp.sum(lq1_ref[...] * lk1_ref[...], axis=-1, keepdims=True))
           - jnp.exp(jnp.sum(lq2_ref[...] * lk2_ref[...], axis=-1, keepdims=True))
           + lambda_init)
    a = acc_ref[0] / l_ref[0] - lam * (acc_ref[1] / l_ref[1])
    a = a * lax.rsqrt(jnp.mean(a * a, axis=0, keepdims=True) + SUBLN_EPS)
    a = a * g_ref[...] * (1.0 - lambda_init)
    o_ref[0] = a.T.astype(o_ref.dtype)


def _attention(proj, vt, bias_tab, lq1, lk1, lq2, lk2, gain, *, lambda_init):
    b, s, _ = proj.shape
    tile = ATTN_TILE
    vec = lambda n: pl.BlockSpec((1, n), lambda bi, h, qi: (0, 0))
    return pl.pallas_call(
        functools.partial(_attn_kernel, tile=tile, lambda_init=lambda_init),
        grid=(b, N_HEADS, s // tile),
        in_specs=[
            pl.BlockSpec((1, tile, HEAD_DIM), lambda bi, h, qi: (bi, qi, h)),
            pl.BlockSpec((1, s, HEAD_DIM), lambda bi, h, qi: (bi, 0, N_HEADS + h)),
            pl.BlockSpec((HEAD_DIM, s), lambda bi, h, qi: (h, bi)),
            pl.BlockSpec((1, 2 * tile, tile), lambda bi, h, qi: (h, 0, 0)),
            vec(QK_DIM), vec(QK_DIM), vec(QK_DIM), vec(QK_DIM),
            pl.BlockSpec((HEAD_DIM, 1), lambda bi, h, qi: (0, 0)),
        ],
        out_specs=pl.BlockSpec((1, tile, HEAD_DIM), lambda bi, h, qi: (bi, qi, h)),
        out_shape=jax.ShapeDtypeStruct((b, s, N_HEADS * HEAD_DIM), BF16),
        scratch_shapes=[
            pltpu.VMEM((2, tile, tile), F32),
            pltpu.VMEM((2, 1, tile), F32),
            pltpu.VMEM((2, 1, tile), F32),
            pltpu.VMEM((2, 1, tile), F32),
            pltpu.VMEM((2, HEAD_DIM, tile), F32),
        ],
        compiler_params=_params("arbitrary", "arbitrary", "arbitrary"),
        name="diff_attention",
    )(proj, proj, vt, bias_tab, lq1, lk1, lq2, lk2, gain)


def _conv_kernel(gb_ref, gc_ref, hc_ref, gc_halo_ref, hc_halo_ref, w_ref, g_ref, o_ref):
    i = pl.program_id(1)
    u = gc_ref[0].astype(F32) * hc_ref[0].astype(F32)
    halo = gc_halo_ref[0].astype(F32) * hc_halo_ref[0].astype(F32)
    halo = halo * (i > 0).astype(F32)
    row = lax.broadcasted_iota(jnp.int32, u.shape, 0)
    u1 = jnp.where(row == 0, halo[7:8], pltpu.roll(u, 1, 0))
    u2 = jnp.where(row == 0, halo[6:7], jnp.where(row == 1, halo[7:8], pltpu.roll(u, 2, 0)))
    w = w_ref[...]
    y = w[0:1] * u2 + w[1:2] * u1 + w[2:3] * u
    c = gb_ref[0].astype(F32) * y
    o_ref[0] = (_rms(c, NORM_EPS) * g_ref[...]).astype(o_ref.dtype)


def _short_conv(proj, conv_w, gain, *, tc):
    b, s, _ = proj.shape
    cw = conv_w.shape[1]
    nblk = 3 * (N_HEADS * HEAD_DIM) // cw
    halo_blocks = tc // SUBLANES
    col = lambda k: pl.BlockSpec((1, tc, cw), lambda bi, i: (bi, i, nblk + k))
    halo = lambda k: pl.BlockSpec(
        (1, SUBLANES, cw), lambda bi, i: (bi, jnp.maximum(i * halo_blocks - 1, 0), nblk + k))
    return pl.pallas_call(
        _conv_kernel,
        grid=(b, s // tc),
        in_specs=[
            col(0), col(1), col(2), halo(1), halo(2),
            pl.BlockSpec((CONV_K, cw), lambda bi, i: (0, 0)),
            pl.BlockSpec((1, cw), lambda bi, i: (0, 0)),
        ],
        out_specs=pl.BlockSpec((1, tc, cw), lambda bi, i: (bi, i, 0)),
        out_shape=jax.ShapeDtypeStruct((b, s, cw), BF16),
        compiler_params=_params("arbitrary", "arbitrary"),
        name="short_conv",
    )(proj, proj, proj, proj, proj, conv_w, gain)


def _out_proj_kernel(a_ref, c_ref, w_ref, x_ref, g_ref, o_ref):
    aw = a_ref.shape[1]
    mix = jnp.dot(a_ref[...], w_ref[:aw, :], preferred_element_type=F32)
    mix = mix + jnp.dot(c_ref[...], w_ref[aw:, :], preferred_element_type=F32)
    o_ref[...] = x_ref[...] + _rms(mix, NORM_EPS) * g_ref[...]


def _out_proj(a, c, w, x, g, *, tm):
    m, d = x.shape
    row = lambda n: pl.BlockSpec((tm, n), lambda i: (i, 0))
    return pl.pallas_call(
        _out_proj_kernel,
        grid=(m // tm,),
        in_specs=[
            row(a.shape[1]), row(c.shape[1]),
            pl.BlockSpec(w.shape, lambda i: (0, 0)),
            row(d),
            pl.BlockSpec((1, d), lambda i: (0, 0)),
        ],
        out_specs=row(d),
        out_shape=jax.ShapeDtypeStruct((m, d), F32),
        compiler_params=_params("arbitrary"),
        name="out_proj",
    )(a, c, w, x, g)


def _ffn_kernel(x_ref, gpre_ref, wg_ref, wu_ref, wd_ref, gpost_ref, o_ref, hn_ref):
    j = pl.program_id(1)

    @pl.when(j == 0)
    def _():
        hn_ref[...] = (_rms(x_ref[...], NORM_EPS) * gpre_ref[...]).astype(BF16)

    hn = hn_ref[...]
    gate = jnp.dot(hn, wg_ref[...], preferred_element_type=F32)
    up = jnp.dot(hn, wu_ref[...], preferred_element_type=F32)
    h = (gate * jax.nn.sigmoid(gate) * up).astype(BF16)
    part = jnp.dot(h, wd_ref[...], preferred_element_type=F32)

    @pl.when(j == 0)
    def _():
        o_ref[...] = part

    @pl.when(j > 0)
    def _():
        o_ref[...] += part

    @pl.when(j == pl.num_programs(1) - 1)
    def _():
        o_ref[...] = x_ref[...] + _rms(o_ref[...], NORM_EPS) * gpost_ref[...]


def _ffn(x, gpre, wg, wu, wd, gpost, *, tm, tf):
    m, d = x.shape
    f = wg.shape[1]
    return pl.pallas_call(
        _ffn_kernel,
        grid=(m // tm, f // tf),
        in_specs=[
            pl.BlockSpec((tm, d), lambda i, j: (i, 0)),
            pl.BlockSpec((1, d), lambda i, j: (0, 0)),
            pl.BlockSpec((d, tf), lambda i, j: (0, j)),
            pl.BlockSpec((d, tf), lambda i, j: (0, j)),
            pl.BlockSpec((tf, d), lambda i, j: (j, 0)),
            pl.BlockSpec((1, d), lambda i, j: (0, 0)),
        ],
        out_specs=pl.BlockSpec((tm, d), lambda i, j: (i, 0)),
        out_shape=jax.ShapeDtypeStruct((m, d), F32),
        scratch_shapes=[pltpu.VMEM((tm, d), BF16)],
        compiler_params=_params("arbitrary", "arbitrary"),
        name="swiglu_ffn",
    )(x, gpre, wg, wu, wd, gpost)


def kernel(x, w_in, w_out, lambda_q1, lambda_k1, lambda_q2, lambda_k2, subln_gain, conv_w, conv_norm_gain, rel_bias, w_gate, w_up, w_down, norm_mix_pre, norm_mix_post, norm_ffn_pre, norm_ffn_post):
    b, s, d = x.shape
    depth = w_in.shape[0]
    assert s % ATTN_TILE == 0 and ATTN_TILE >= MAX_DISTANCE

    w_in, w_out, w_gate, w_up, w_down = (w.astype(BF16) for w in (w_in, w_out, w_gate, w_up, w_down))
    bias_tab = _bias_tiles(rel_bias, ATTN_TILE)
    row = lambda v: v.reshape(1, -1)

    xf = x.reshape(b * s, d)
    for l in range(depth):
        lambda_init = 0.8 - 0.6 * math.exp(-0.3 * l)
        proj, vt = _norm_matmul(xf, row(norm_mix_pre[l]), w_in[l], tm=1024, tn=N_HEADS * HEAD_DIM,
                                v_block=2)
        proj = proj.reshape(b, s, -1)
        a = _attention(proj, vt, bias_tab, row(lambda_q1[l]), row(lambda_k1[l]), row(lambda_q2[l]),
                       row(lambda_k2[l]), subln_gain[l].reshape(-1, 1), lambda_init=lambda_init)
        c = _short_conv(proj, conv_w[l], row(conv_norm_gain[l]), tc=512)
        xf = _out_proj(a.reshape(b * s, -1), c.reshape(b * s, -1), w_out[l], xf,
                       row(norm_mix_post[l]), tm=512)
        xf = _ffn(xf, row(norm_ffn_pre[l]), w_gate[l], w_up[l], w_down[l], row(norm_ffn_post[l]),
                  tm=512, tf=512)
    return xf.reshape(b, s, d)
```

```python
import functools
import math

import jax
import jax.numpy as jnp
from jax import lax
from jax.experimental import pallas as pl
from jax.experimental.pallas import tpu as pltpu

N_HEADS = 8
HEAD_DIM = 128
QK_DIM = 64
CONV_K = 3
N_BUCKETS = 32
MAX_DISTANCE = 128
NORM_EPS = 1e-6
SUBLN_EPS = 1e-5

LANES = 128
SUBLANES = 8
VMEM_LIMIT = 56 * 1024 * 1024

ATTN_K_TILE = 512
MASK_VALUE = -1e30
LOG2E = math.log2(math.e)

F32 = jnp.float32
BF16 = jnp.bfloat16


def _params(*sem):
    return pltpu.CompilerParams(dimension_semantics=sem, vmem_limit_bytes=VMEM_LIMIT)


def _rms(x, eps):
    return x * lax.rsqrt(jnp.mean(x * x, axis=-1, keepdims=True) + eps)


def _bias_kernel(rb_ref, o_ref, *, kt):
    h = pl.program_id(0)
    shape = o_ref.shape[1:]
    key = pl.program_id(1) * kt + lax.broadcasted_iota(jnp.int32, shape, 0)
    dist = kt + lax.broadcasted_iota(jnp.int32, shape, 1) - key
    n = jnp.maximum(dist, 0)
    max_exact = N_BUCKETS // 2
    nf = jnp.maximum(n, max_exact).astype(F32)
    large = max_exact + (jnp.log(nf / max_exact) / math.log(MAX_DISTANCE / max_exact)
                         * (N_BUCKETS - max_exact)).astype(jnp.int32)
    large = jnp.minimum(large, N_BUCKETS - 1)
    bucket = jnp.where(n < max_exact, n, large)
    val = jnp.zeros(shape, F32)
    for b in range(N_BUCKETS):
        val = jnp.where(bucket == b, rb_ref[b, h], val)
    val = (val - rb_ref[N_BUCKETS - 1, h]) * LOG2E
    o_ref[0] = jnp.where(dist >= 0, val, MASK_VALUE)


def _bias_tiles(rel_bias, kt):
    qt = 2 * kt
    return pl.pallas_call(
        functools.partial(_bias_kernel, kt=kt),
        grid=(N_HEADS, 3),
        in_specs=[pl.BlockSpec(memory_space=pltpu.SMEM)],
        out_specs=pl.BlockSpec((1, kt, qt), lambda h, t: (h, t, 0)),
        out_shape=jax.ShapeDtypeStruct((N_HEADS, 3 * kt, qt), F32),
        compiler_params=_params("arbitrary", "arbitrary"),
        name="rel_bias_tiles",
    )(rel_bias)


def _norm_matmul_kernel(x_ref, g_ref, w_ref, o_ref, vt_ref, hn_ref, *, v_block):
    j = pl.program_id(1)

    @pl.when(j == 0)
    def _():
        hn_ref[...] = (_rms(x_ref[...], NORM_EPS) * g_ref[...]).astype(BF16)

    r = jnp.dot(hn_ref[...], w_ref[...], preferred_element_type=F32)
    o_ref[...] = r.astype(o_ref.dtype)

    @pl.when(j == v_block)
    def _():
        vt_ref[...] = r.T.astype(vt_ref.dtype)


def _norm_matmul(x, g, w, *, tm, tn, v_block):
    m, d = x.shape
    n = w.shape[1]
    return pl.pallas_call(
        functools.partial(_norm_matmul_kernel, v_block=v_block),
        grid=(m // tm, n // tn),
        in_specs=[
            pl.BlockSpec((tm, d), lambda i, j: (i, 0)),
            pl.BlockSpec((1, d), lambda i, j: (0, 0)),
            pl.BlockSpec((d, tn), lambda i, j: (0, j)),
        ],
        out_specs=[
            pl.BlockSpec((tm, tn), lambda i, j: (i, j)),
            pl.BlockSpec((tn, tm), lambda i, j: (0, i)),
        ],
        out_shape=[
            jax.ShapeDtypeStruct((m, n), BF16),
            jax.ShapeDtypeStruct((tn, m), BF16),
        ],
        scratch_shapes=[pltpu.VMEM((tm, d), BF16)],
        compiler_params=_params("arbitrary", "arbitrary"),
        name="norm_in_proj",
    )(x, g, w)


def _attn_kernel(q_ref, k_ref, vt_ref, bias_ref, lq1_ref, lk1_ref, lq2_ref, lk2_ref, g_ref,
                 o_ref, s_ref, m_ref, alpha_ref, l_ref, acc_ref, *, kt, lambda_init):
    qi = pl.program_id(2)
    qt = 2 * kt
    all_q, late_q = slice(0, qt), slice(kt, qt)

    q = (q_ref[0].astype(F32) * (QK_DIM ** -0.5 * LOG2E)).astype(BF16)
    lane = lax.broadcasted_iota(jnp.int32, q.shape, 1)
    zero = jnp.zeros_like(q)
    q_halves = (jnp.where(lane < QK_DIM, q, zero), jnp.where(lane >= QK_DIM, q, zero))

    m_ref[...] = jnp.full(m_ref.shape, MASK_VALUE, F32)
    l_ref[...] = jnp.zeros(l_ref.shape, F32)
    acc_ref[...] = jnp.zeros(acc_ref.shape, F32)

    def qk(start, qs):
        k_tile = k_ref[0, pl.ds(start, kt), :]
        return [lax.dot_general(k_tile, q_halves[c][qs], (((1,), (1,)), ((), ())),
                                preferred_element_type=F32) for c in range(2)]

    def put_scores(scores, bias, qs):
        for c, s in enumerate(scores):
            if bias is not None:
                s = s + bias
            m_prev = m_ref[c, :, qs]
            m_new = jnp.maximum(m_prev, jnp.max(s, axis=0, keepdims=True))
            alpha_ref[c, :, qs] = jnp.exp2(m_prev - m_new)
            m_ref[c, :, qs] = m_new
            s_ref[c, :, qs] = s

    def accumulate(start, qs):
        vt = vt_ref[:, pl.ds(start, kt)]
        for c in range(2):
            alpha = alpha_ref[c, :, qs]
            p = jnp.exp2(s_ref[c, :, qs] - m_ref[c, :, qs])
            l_ref[c, :, qs] = alpha * l_ref[c, :, qs] + jnp.sum(p, axis=0, keepdims=True)
            acc_ref[c, :, qs] = alpha * acc_ref[c, :, qs] + jnp.dot(
                vt, p.astype(BF16), preferred_element_type=F32)

    tile_start = lambda t: pl.multiple_of(t * kt, kt)
    put_scores(qk(tile_start(2 * qi + 1), late_q), bias_ref[0, 2 * kt:, late_q], late_q)

    nxt = qk(tile_start(2 * qi), all_q)
    accumulate(tile_start(2 * qi + 1), late_q)
    put_scores(nxt, bias_ref[0, kt:2 * kt, :], all_q)

    @pl.when(qi >= 1)
    def _():
        nxt = qk(tile_start(2 * qi - 1), all_q)
        accumulate(tile_start(2 * qi), all_q)
        put_scores(nxt, bias_ref[0, :kt, :], all_q)

    n_far = jnp.maximum(2 * qi - 1, 0)

    def far_tile(j, cur):
        nxt = qk(tile_start(j), all_q)
        accumulate(tile_start(cur), all_q)
        put_scores(nxt, None, all_q)
        return j

    last = lax.fori_loop(0, n_far, far_tile, n_far)
    accumulate(tile_start(last), all_q)

    lam = (jnp.exp(jnp.sum(lq1_ref[...] * lk1_ref[...], axis=-1, keepdims=True))
           - jnp.exp(jnp.sum(lq2_ref[...] * lk2_ref[...], axis=-1, keepdims=True))
           + lambda_init)
    a = acc_ref[0] / l_ref[0] - lam * (acc_ref[1] / l_ref[1])
    a = a * lax.rsqrt(jnp.mean(a * a, axis=0, keepdims=True) + SUBLN_EPS)
    a = a * g_ref[...] * (1.0 - lambda_init)
    o_ref[0] = a.T.astype(o_ref.dtype)


def _attention(proj, vt, bias_tab, lq1, lk1, lq2, lk2, gain, *, lambda_init):
    b, s, _ = proj.shape
    kt = ATTN_K_TILE
    qt = 2 * kt
    vec = lambda n: pl.BlockSpec((1, n), lambda bi, h, qi: (0, 0))
    return pl.pallas_call(
        functools.partial(_attn_kernel, kt=kt, lambda_init=lambda_init),
        grid=(b, N_HEADS, s // qt),
        in_specs=[
            pl.BlockSpec((1, qt, HEAD_DIM), lambda bi, h, qi: (bi, qi, h)),
            pl.BlockSpec((1, s, HEAD_DIM), lambda bi, h, qi: (bi, 0, N_HEADS + h)),
            pl.BlockSpec((HEAD_DIM, s), lambda bi, h, qi: (h, bi)),
            pl.BlockSpec((1, 3 * kt, qt), lambda bi, h, qi: (h, 0, 0)),
            vec(QK_DIM), vec(QK_DIM), vec(QK_DIM), vec(QK_DIM),
            pl.BlockSpec((HEAD_DIM, 1), lambda bi, h, qi: (0, 0)),
        ],
        out_specs=pl.BlockSpec((1, qt, HEAD_DIM), lambda bi, h, qi: (bi, qi, h)),
        out_shape=jax.ShapeDtypeStruct((b, s, N_HEADS * HEAD_DIM), BF16),
        scratch_shapes=[
            pltpu.VMEM((2, kt, qt), F32),
            pltpu.VMEM((2, 1, qt), F32),
            pltpu.VMEM((2, 1, qt), F32),
            pltpu.VMEM((2, 1, qt), F32),
            pltpu.VMEM((2, HEAD_DIM, qt), F32),
        ],
        compiler_params=_params("arbitrary", "arbitrary", "arbitrary"),
        name="diff_attention",
    )(proj, proj, vt, bias_tab, lq1, lk1, lq2, lk2, gain)


def _conv_kernel(gb_ref, gc_ref, hc_ref, gc_halo_ref, hc_halo_ref, w_ref, g_ref, o_ref):
    i = pl.program_id(1)
    u = gc_ref[0].astype(F32) * hc_ref[0].astype(F32)
    halo = gc_halo_ref[0].astype(F32) * hc_halo_ref[0].astype(F32)
    halo = halo * (i > 0).astype(F32)
    row = lax.broadcasted_iota(jnp.int32, u.shape, 0)
    u1 = jnp.where(row == 0, halo[7:8], pltpu.roll(u, 1, 0))
    u2 = jnp.where(row == 0, halo[6:7], jnp.where(row == 1, halo[7:8], pltpu.roll(u, 2, 0)))
    w = w_ref[...]
    y = w[0:1] * u2 + w[1:2] * u1 + w[2:3] * u
    c = gb_ref[0].astype(F32) * y
    o_ref[0] = (_rms(c, NORM_EPS) * g_ref[...]).astype(o_ref.dtype)


def _short_conv(proj, conv_w, gain, *, tc):
    b, s, _ = proj.shape
    cw = conv_w.shape[1]
    nblk = 3 * (N_HEADS * HEAD_DIM) // cw
    halo_blocks = tc // SUBLANES
    col = lambda k: pl.BlockSpec((1, tc, cw), lambda bi, i: (bi, i, nblk + k))
    halo = lambda k: pl.BlockSpec(
        (1, SUBLANES, cw), lambda bi, i: (bi, jnp.maximum(i * halo_blocks - 1, 0), nblk + k))
    return pl.pallas_call(
        _conv_kernel,
        grid=(b, s // tc),
        in_specs=[
            col(0), col(1), col(2), halo(1), halo(2),
            pl.BlockSpec((CONV_K, cw), lambda bi, i: (0, 0)),
            pl.BlockSpec((1, cw), lambda bi, i: (0, 0)),
        ],
        out_specs=pl.BlockSpec((1, tc, cw), lambda bi, i: (bi, i, 0)),
        out_shape=jax.ShapeDtypeStruct((b, s, cw), BF16),
        compiler_params=_params("arbitrary", "arbitrary"),
        name="short_conv",
    )(proj, proj, proj, proj, proj, conv_w, gain)


def _out_proj_kernel(a_ref, c_ref, w_ref, x_ref, g_ref, o_ref):
    aw = a_ref.shape[1]
    mix = jnp.dot(a_ref[...], w_ref[:aw, :], preferred_element_type=F32)
    mix = mix + jnp.dot(c_ref[...], w_ref[aw:, :], preferred_element_type=F32)
    o_ref[...] = x_ref[...] + _rms(mix, NORM_EPS) * g_ref[...]


def _out_proj(a, c, w, x, g, *, tm):
    m, d = x.shape
    row = lambda n: pl.BlockSpec((tm, n), lambda i: (i, 0))
    return pl.pallas_call(
        _out_proj_kernel,
        grid=(m // tm,),
        in_specs=[
            row(a.shape[1]), row(c.shape[1]),
            pl.BlockSpec(w.shape, lambda i: (0, 0)),
            row(d),
            pl.BlockSpec((1, d), lambda i: (0, 0)),
        ],
        out_specs=row(d),
        out_shape=jax.ShapeDtypeStruct((m, d), F32),
        compiler_params=_params("arbitrary"),
        name="out_proj",
    )(a, c, w, x, g)


def _ffn_kernel(x_ref, gpre_ref, wg_ref, wu_ref, wd_ref, gpost_ref, o_ref, hn_ref):
    j = pl.program_id(1)

    @pl.when(j == 0)
    def _():
        hn_ref[...] = (_rms(x_ref[...], NORM_EPS) * gpre_ref[...]).astype(BF16)

    hn = hn_ref[...]
    gate = jnp.dot(hn, wg_ref[...], preferred_element_type=F32)
    up = jnp.dot(hn, wu_ref[...], preferred_element_type=F32)
    h = (gate * jax.nn.sigmoid(gate) * up).astype(BF16)
    part = jnp.dot(h, wd_ref[...], preferred_element_type=F32)

    @pl.when(j == 0)
    def _():
        o_ref[...] = part

    @pl.when(j > 0)
    def _():
        o_ref[...] += part

    @pl.when(j == pl.num_programs(1) - 1)
    def _():
        o_ref[...] = x_ref[...] + _rms(o_ref[...], NORM_EPS) * gpost_ref[...]


def _ffn(x, gpre, wg, wu, wd, gpost, *, tm, tf):
    m, d = x.shape
    f = wg.shape[1]
    return pl.pallas_call(
        _ffn_kernel,
        grid=(m // tm, f // tf),
        in_specs=[
            pl.BlockSpec((tm, d), lambda i, j: (i, 0)),
            pl.BlockSpec((1, d), lambda i, j: (0, 0)),
            pl.BlockSpec((d, tf), lambda i, j: (0, j)),
            pl.BlockSpec((d, tf), lambda i, j: (0, j)),
            pl.BlockSpec((tf, d), lambda i, j: (j, 0)),
            pl.BlockSpec((1, d), lambda i, j: (0, 0)),
        ],
        out_specs=pl.BlockSpec((tm, d), lambda i, j: (i, 0)),
        out_shape=jax.ShapeDtypeStruct((m, d), F32),
        scratch_shapes=[pltpu.VMEM((tm, d), BF16)],
        compiler_params=_params("arbitrary", "arbitrary"),
        name="swiglu_ffn",
    )(x, gpre, wg, wu, wd, gpost)


def kernel(x, w_in, w_out, lambda_q1, lambda_k1, lambda_q2, lambda_k2, subln_gain, conv_w, conv_norm_gain, rel_bias, w_gate, w_up, w_down, norm_mix_pre, norm_mix_post, norm_ffn_pre, norm_ffn_post):
    b, s, d = x.shape
    depth = w_in.shape[0]
    assert s % (2 * ATTN_K_TILE) == 0 and ATTN_K_TILE >= MAX_DISTANCE

    w_in, w_out, w_gate, w_up, w_down = (w.astype(BF16) for w in (w_in, w_out, w_gate, w_up, w_down))
    bias_tab = _bias_tiles(rel_bias, ATTN_K_TILE)
    row = lambda v: v.reshape(1, -1)

    xf = x.reshape(b * s, d)
    for l in range(depth):
        lambda_init = 0.8 - 0.6 * math.exp(-0.3 * l)
        proj, vt = _norm_matmul(xf, row(norm_mix_pre[l]), w_in[l], tm=1024, tn=N_HEADS * HEAD_DIM,
                                v_block=2)
        proj = proj.reshape(b, s, -1)
        a = _attention(proj, vt, bias_tab, row(lambda_q1[l]), row(lambda_k1[l]), row(lambda_q2[l]),
                       row(lambda_k2[l]), subln_gain[l].reshape(-1, 1), lambda_init=lambda_init)
        c = _short_conv(proj, conv_w[l], row(conv_norm_gain[l]), tc=512)
        xf = _out_proj(a.reshape(b * s, -1), c.reshape(b * s, -1), w_out[l], xf,
                       row(norm_mix_post[l]), tm=512)
        xf = _ffn(xf, row(norm_ffn_pre[l]), w_gate[l], w_up[l], w_down[l], row(norm_ffn_post[l]),
                  tm=512, tf=512)
    return xf.reshape(b, s, d)
```

```python
import functools
import math

import jax
import jax.numpy as jnp
from jax import lax
from jax.experimental import pallas as pl
from jax.experimental.pallas import tpu as pltpu

N_HEADS = 8
HEAD_DIM = 128
QK_DIM = 64
CONV_K = 3
N_BUCKETS = 32
MAX_DISTANCE = 128
NORM_EPS = 1e-6
SUBLN_EPS = 1e-5

LANES = 128
SUBLANES = 8
VMEM_LIMIT = 56 * 1024 * 1024

ATTN_K_TILE = 512
QUERY_BLOCK = 256
MASK_VALUE = -1e30
LOG2E = math.log2(math.e)

F32 = jnp.float32
BF16 = jnp.bfloat16


def _params(*sem):
    return pltpu.CompilerParams(dimension_semantics=sem, vmem_limit_bytes=VMEM_LIMIT)


def _rms(x, eps):
    return x * lax.rsqrt(jnp.mean(x * x, axis=-1, keepdims=True) + eps)


def _bias_kernel(rb_ref, o_ref, *, kt):
    h = pl.program_id(0)
    shape = o_ref.shape[1:]
    key = pl.program_id(1) * kt + lax.broadcasted_iota(jnp.int32, shape, 0)
    dist = kt + lax.broadcasted_iota(jnp.int32, shape, 1) - key
    n = jnp.maximum(dist, 0)
    max_exact = N_BUCKETS // 2
    nf = jnp.maximum(n, max_exact).astype(F32)
    large = max_exact + (jnp.log(nf / max_exact) / math.log(MAX_DISTANCE / max_exact)
                         * (N_BUCKETS - max_exact)).astype(jnp.int32)
    large = jnp.minimum(large, N_BUCKETS - 1)
    bucket = jnp.where(n < max_exact, n, large)
    val = jnp.zeros(shape, F32)
    for b in range(N_BUCKETS):
        val = jnp.where(bucket == b, rb_ref[b, h], val)
    val = (val - rb_ref[N_BUCKETS - 1, h]) * LOG2E
    o_ref[0] = jnp.where(dist >= 0, val, MASK_VALUE)


def _bias_tiles(rel_bias, kt):
    qt = 2 * kt
    return pl.pallas_call(
        functools.partial(_bias_kernel, kt=kt),
        grid=(N_HEADS, 3),
        in_specs=[pl.BlockSpec(memory_space=pltpu.SMEM)],
        out_specs=pl.BlockSpec((1, kt, qt), lambda h, t: (h, t, 0)),
        out_shape=jax.ShapeDtypeStruct((N_HEADS, 3 * kt, qt), F32),
        compiler_params=_params("arbitrary", "arbitrary"),
        name="rel_bias_tiles",
    )(rel_bias)


def _norm_matmul_kernel(x_ref, g_ref, w_ref, o_ref, vt_ref, hn_ref, *, v_block):
    j = pl.program_id(1)

    @pl.when(j == 0)
    def _():
        hn_ref[...] = (_rms(x_ref[...], NORM_EPS) * g_ref[...]).astype(BF16)

    r = jnp.dot(hn_ref[...], w_ref[...], preferred_element_type=F32)
    o_ref[...] = r.astype(o_ref.dtype)

    @pl.when(j == v_block)
    def _():
        vt_ref[...] = r.T.astype(vt_ref.dtype)


def _norm_matmul(x, g, w, *, tm, tn, v_block):
    m, d = x.shape
    n = w.shape[1]
    return pl.pallas_call(
        functools.partial(_norm_matmul_kernel, v_block=v_block),
        grid=(m // tm, n // tn),
        in_specs=[
            pl.BlockSpec((tm, d), lambda i, j: (i, 0)),
            pl.BlockSpec((1, d), lambda i, j: (0, 0)),
            pl.BlockSpec((d, tn), lambda i, j: (0, j)),
        ],
        out_specs=[
            pl.BlockSpec((tm, tn), lambda i, j: (i, j)),
            pl.BlockSpec((tn, tm), lambda i, j: (0, i)),
        ],
        out_shape=[
            jax.ShapeDtypeStruct((m, n), BF16),
            jax.ShapeDtypeStruct((tn, m), BF16),
        ],
        scratch_shapes=[pltpu.VMEM((tm, d), BF16)],
        compiler_params=_params("arbitrary", "arbitrary"),
        name="norm_in_proj",
    )(x, g, w)


def _attn_kernel(q_ref, k_ref, vt_ref, bias_ref, lq1_ref, lk1_ref, lq2_ref, lk2_ref, g_ref,
                 o_ref, s_ref, m_ref, alpha_ref, l_ref, acc_ref, *, kt, lambda_init):
    qi = pl.program_id(2)
    qt = 2 * kt
    all_q, late_q = slice(0, qt), slice(kt, qt)

    q = (q_ref[0].astype(F32) * (QK_DIM ** -0.5 * LOG2E)).astype(BF16)
    lane = lax.broadcasted_iota(jnp.int32, q.shape, 1)
    zero = jnp.zeros_like(q)
    q_halves = (jnp.where(lane < QK_DIM, q, zero), jnp.where(lane >= QK_DIM, q, zero))

    m_ref[...] = jnp.full(m_ref.shape, MASK_VALUE, F32)
    l_ref[...] = jnp.zeros(l_ref.shape, F32)
    acc_ref[...] = jnp.zeros(acc_ref.shape, F32)

    def qk(t, qs):
        k_tile = k_ref[0, pl.ds(pl.multiple_of(t * kt, kt), kt), :]
        return [lax.dot_general(k_tile, q_halves[c][qs], (((1,), (1,)), ((), ())),
                                preferred_element_type=F32) for c in range(2)]

    def put_scores(scores, slot, bias, qs):
        for c, s in enumerate(scores):
            if bias is not None:
                s = s + bias
            m_prev = m_ref[1 - slot, c, :, qs]
            m_new = jnp.maximum(m_prev, jnp.max(s, axis=0, keepdims=True))
            alpha_ref[slot, c, :, qs] = jnp.exp2(m_prev - m_new)
            m_ref[slot, c, :, qs] = m_new
            s_ref[slot, c, :, qs] = s

    def accumulate(t, slot, qs):
        vt = vt_ref[:, pl.ds(pl.multiple_of(t * kt, kt), kt)]
        for c in range(2):
            alpha = alpha_ref[slot, c, :, qs]
            p = jnp.exp2(s_ref[slot, c, :, qs] - m_ref[slot, c, :, qs])
            l_ref[c, :, qs] = alpha * l_ref[c, :, qs] + jnp.sum(p, axis=0, keepdims=True)
            acc_ref[c, :, qs] = alpha * acc_ref[c, :, qs] + jnp.dot(
                vt, p.astype(BF16), preferred_element_type=F32)

    def stage(nxt, slot, bias, cur, cur_qs):
        for lo in range(0, qt, QUERY_BLOCK):
            qs = slice(lo, lo + QUERY_BLOCK)
            scores = qk(nxt, qs)
            if lo >= cur_qs.start:
                accumulate(cur, 1 - slot, qs)
            put_scores(scores, slot, None if bias is None else bias[:, qs], qs)

    put_scores(qk(2 * qi + 1, late_q), 0, bias_ref[0, 2 * kt:, late_q], late_q)
    stage(2 * qi, 1, bias_ref[0, kt:2 * kt, :], 2 * qi + 1, late_q)

    @pl.when(qi >= 1)
    def _():
        stage(2 * qi - 1, 0, bias_ref[0, :kt, :], 2 * qi, all_q)

    def far_pair(p, in_slot0):
        stage(2 * p, 1, None, in_slot0, all_q)
        stage(2 * p + 1, 0, None, 2 * p, all_q)
        return 2 * p + 1

    in_slot0 = lax.fori_loop(0, qi - 1, far_pair, jnp.maximum(2 * qi - 1, 0))

    @pl.when(qi >= 1)
    def _():
        stage(2 * qi - 2, 1, None, in_slot0, all_q)

    accumulate(jnp.maximum(2 * qi - 2, 0), 1, all_q)

    lam = (jnp.exp(jnp.sum(lq1_ref[...] * lk1_ref[...], axis=-1, keepdims=True))
           - jnp.exp(jnp.sum(lq2_ref[...] * lk2_ref[...], axis=-1, keepdims=True))
           + lambda_init)
    a = acc_ref[0] / l_ref[0] - lam * (acc_ref[1] / l_ref[1])
    a = a * lax.rsqrt(jnp.mean(a * a, axis=0, keepdims=True) + SUBLN_EPS)
    a = a * g_ref[...] * (1.0 - lambda_init)
    o_ref[0] = a.T.astype(o_ref.dtype)


def _attention(proj, vt, bias_tab, lq1, lk1, lq2, lk2, gain, *, lambda_init):
    b, s, _ = proj.shape
    kt = ATTN_K_TILE
    qt = 2 * kt
    vec = lambda n: pl.BlockSpec((1, n), lambda bi, h, qi: (0, 0))
    return pl.pallas_call(
        functools.partial(_attn_kernel, kt=kt, lambda_init=lambda_init),
        grid=(b, N_HEADS, s // qt),
        in_specs=[
            pl.BlockSpec((1, qt, HEAD_DIM), lambda bi, h, qi: (bi, qi, h)),
            pl.BlockSpec((1, s, HEAD_DIM), lambda bi, h, qi: (bi, 0, N_HEADS + h)),
            pl.BlockSpec((HEAD_DIM, s), lambda bi, h, qi: (h, bi)),
            pl.BlockSpec((1, 3 * kt, qt), lambda bi, h, qi: (h, 0, 0)),
            vec(QK_DIM), vec(QK_DIM), vec(QK_DIM), vec(QK_DIM),
            pl.BlockSpec((HEAD_DIM, 1), lambda bi, h, qi: (0, 0)),
        ],
        out_specs=pl.BlockSpec((1, qt, HEAD_DIM), lambda bi, h, qi: (bi, qi, h)),
        out_shape=jax.ShapeDtypeStruct((b, s, N_HEADS * HEAD_DIM), BF16),
        scratch_shapes=[
            pltpu.VMEM((2, 2, kt, qt), F32),
            pltpu.VMEM((2, 2, 1, qt), F32),
            pltpu.VMEM((2, 2, 1, qt), F32),
            pltpu.VMEM((2, 1, qt), F32),
            pltpu.VMEM((2, HEAD_DIM, qt), F32),
        ],
        compiler_params=_params("arbitrary", "arbitrary", "arbitrary"),
        name="diff_attention",
    )(proj, proj, vt, bias_tab, lq1, lk1, lq2, lk2, gain)


def _conv_kernel(gb_ref, gc_ref, hc_ref, gc_halo_ref, hc_halo_ref, w_ref, g_ref, o_ref):
    i = pl.program_id(1)
    u = gc_ref[0].astype(F32) * hc_ref[0].astype(F32)
    halo = gc_halo_ref[0].astype(F32) * hc_halo_ref[0].astype(F32)
    halo = halo * (i > 0).astype(F32)
    row = lax.broadcasted_iota(jnp.int32, u.shape, 0)
    u1 = jnp.where(row == 0, halo[7:8], pltpu.roll(u, 1, 0))
    u2 = jnp.where(row == 0, halo[6:7], jnp.where(row == 1, halo[7:8], pltpu.roll(u, 2, 0)))
    w = w_ref[...]
    y = w[0:1] * u2 + w[1:2] * u1 + w[2:3] * u
    c = gb_ref[0].astype(F32) * y
    o_ref[0] = (_rms(c, NORM_EPS) * g_ref[...]).astype(o_ref.dtype)


def _short_conv(proj, conv_w, gain, *, tc):
    b, s, _ = proj.shape
    cw = conv_w.shape[1]
    nblk = 3 * (N_HEADS * HEAD_DIM) // cw
    halo_blocks = tc // SUBLANES
    col = lambda k: pl.BlockSpec((1, tc, cw), lambda bi, i: (bi, i, nblk + k))
    halo = lambda k: pl.BlockSpec(
        (1, SUBLANES, cw), lambda bi, i: (bi, jnp.maximum(i * halo_blocks - 1, 0), nblk + k))
    return pl.pallas_call(
        _conv_kernel,
        grid=(b, s // tc),
        in_specs=[
            col(0), col(1), col(2), halo(1), halo(2),
            pl.BlockSpec((CONV_K, cw), lambda bi, i: (0, 0)),
            pl.BlockSpec((1, cw), lambda bi, i: (0, 0)),
        ],
        out_specs=pl.BlockSpec((1, tc, cw), lambda bi, i: (bi, i, 0)),
        out_shape=jax.ShapeDtypeStruct((b, s, cw), BF16),
        compiler_params=_params("arbitrary", "arbitrary"),
        name="short_conv",
    )(proj, proj, proj, proj, proj, conv_w, gain)


def _out_proj_kernel(a_ref, c_ref, w_ref, x_ref, g_ref, o_ref):
    aw = a_ref.shape[1]
    mix = jnp.dot(a_ref[...], w_ref[:aw, :], preferred_element_type=F32)
    mix = mix + jnp.dot(c_ref[...], w_ref[aw:, :], preferred_element_type=F32)
    o_ref[...] = x_ref[...] + _rms(mix, NORM_EPS) * g_ref[...]


def _out_proj(a, c, w, x, g, *, tm):
    m, d = x.shape
    row = lambda n: pl.BlockSpec((tm, n), lambda i: (i, 0))
    return pl.pallas_call(
        _out_proj_kernel,
        grid=(m // tm,),
        in_specs=[
            row(a.shape[1]), row(c.shape[1]),
            pl.BlockSpec(w.shape, lambda i: (0, 0)),
            row(d),
            pl.BlockSpec((1, d), lambda i: (0, 0)),
        ],
        out_specs=row(d),
        out_shape=jax.ShapeDtypeStruct((m, d), F32),
        compiler_params=_params("arbitrary"),
        name="out_proj",
    )(a, c, w, x, g)


def _ffn_kernel(x_ref, gpre_ref, wg_ref, wu_ref, wd_ref, gpost_ref, o_ref, hn_ref):
    j = pl.program_id(1)

    @pl.when(j == 0)
    def _():
        hn_ref[...] = (_rms(x_ref[...], NORM_EPS) * gpre_ref[...]).astype(BF16)

    hn = hn_ref[...]
    gate = jnp.dot(hn, wg_ref[...], preferred_element_type=F32)
    up = jnp.dot(hn, wu_ref[...], preferred_element_type=F32)
    h = (gate * jax.nn.sigmoid(gate) * up).astype(BF16)
    part = jnp.dot(h, wd_ref[...], preferred_element_type=F32)

    @pl.when(j == 0)
    def _():
        o_ref[...] = part

    @pl.when(j > 0)
    def _():
        o_ref[...] += part

    @pl.when(j == pl.num_programs(1) - 1)
    def _():
        o_ref[...] = x_ref[...] + _rms(o_ref[...], NORM_EPS) * gpost_ref[...]


def _ffn(x, gpre, wg, wu, wd, gpost, *, tm, tf):
    m, d = x.shape
    f = wg.shape[1]
    return pl.pallas_call(
        _ffn_kernel,
        grid=(m // tm, f // tf),
        in_specs=[
            pl.BlockSpec((tm, d), lambda i, j: (i, 0)),
            pl.BlockSpec((1, d), lambda i, j: (0, 0)),
            pl.BlockSpec((d, tf), lambda i, j: (0, j)),
            pl.BlockSpec((d, tf), lambda i, j: (0, j)),
            pl.BlockSpec((tf, d), lambda i, j: (j, 0)),
            pl.BlockSpec((1, d), lambda i, j: (0, 0)),
        ],
        out_specs=pl.BlockSpec((tm, d), lambda i, j: (i, 0)),
        out_shape=jax.ShapeDtypeStruct((m, d), F32),
        scratch_shapes=[pltpu.VMEM((tm, d), BF16)],
        compiler_params=_params("arbitrary", "arbitrary"),
        name="swiglu_ffn",
    )(x, gpre, wg, wu, wd, gpost)


def kernel(x, w_in, w_out, lambda_q1, lambda_k1, lambda_q2, lambda_k2, subln_gain, conv_w, conv_norm_gain, rel_bias, w_gate, w_up, w_down, norm_mix_pre, norm_mix_post, norm_ffn_pre, norm_ffn_post):
    b, s, d = x.shape
    depth = w_in.shape[0]
    assert s % (2 * ATTN_K_TILE) == 0 and ATTN_K_TILE >= MAX_DISTANCE

    w_in, w_out, w_gate, w_up, w_down = (w.astype(BF16) for w in (w_in, w_out, w_gate, w_up, w_down))
    bias_tab = _bias_tiles(rel_bias, ATTN_K_TILE)
    row = lambda v: v.reshape(1, -1)

    xf = x.reshape(b * s, d)
    for l in range(depth):
        lambda_init = 0.8 - 0.6 * math.exp(-0.3 * l)
        proj, vt = _norm_matmul(xf, row(norm_mix_pre[l]), w_in[l], tm=1024, tn=N_HEADS * HEAD_DIM,
                                v_block=2)
        proj = proj.reshape(b, s, -1)
        a = _attention(proj, vt, bias_tab, row(lambda_q1[l]), row(lambda_k1[l]), row(lambda_q2[l]),
                       row(lambda_k2[l]), subln_gain[l].reshape(-1, 1), lambda_init=lambda_init)
        c = _short_conv(proj, conv_w[l], row(conv_norm_gain[l]), tc=512)
        xf = _out_proj(a.reshape(b * s, -1), c.reshape(b * s, -1), w_out[l], xf,
                       row(norm_mix_post[l]), tm=512)
        xf = _ffn(xf, row(norm_ffn_pre[l]), w_gate[l], w_up[l], w_down[l], row(norm_ffn_post[l]),
                  tm=512, tf=512)
    return xf.reshape(b, s, d)
```

```python
import functools
import math

import jax
import jax.numpy as jnp
from jax import lax
from jax.experimental import pallas as pl
from jax.experimental.pallas import tpu as pltpu

N_HEADS = 8
HEAD_DIM = 128
QK_DIM = 64
CONV_K = 3
N_BUCKETS = 32
MAX_DISTANCE = 128
NORM_EPS = 1e-6
SUBLN_EPS = 1e-5

LANES = 128
SUBLANES = 8
VMEM_LIMIT = 56 * 1024 * 1024

ATTN_K_TILE = 512
QUERY_BLOCK = 256
MASK_VALUE = -1e30
LOG2E = math.log2(math.e)

F32 = jnp.float32
BF16 = jnp.bfloat16


def _params(*sem):
    return pltpu.CompilerParams(dimension_semantics=sem, vmem_limit_bytes=VMEM_LIMIT)


def _rms(x, eps):
    return x * lax.rsqrt(jnp.mean(x * x, axis=-1, keepdims=True) + eps)


def _bias_kernel(rb_ref, o_ref, *, kt):
    h = pl.program_id(0)
    shape = o_ref.shape[1:]
    key = pl.program_id(1) * kt + lax.broadcasted_iota(jnp.int32, shape, 0)
    dist = kt + lax.broadcasted_iota(jnp.int32, shape, 1) - key
    n = jnp.maximum(dist, 0)
    max_exact = N_BUCKETS // 2
    nf = jnp.maximum(n, max_exact).astype(F32)
    large = max_exact + (jnp.log(nf / max_exact) / math.log(MAX_DISTANCE / max_exact)
                         * (N_BUCKETS - max_exact)).astype(jnp.int32)
    large = jnp.minimum(large, N_BUCKETS - 1)
    bucket = jnp.where(n < max_exact, n, large)
    val = jnp.zeros(shape, F32)
    for b in range(N_BUCKETS):
        val = jnp.where(bucket == b, rb_ref[b, h], val)
    val = (val - rb_ref[N_BUCKETS - 1, h]) * LOG2E
    o_ref[0] = jnp.where(dist >= 0, val, MASK_VALUE)


def _bias_tiles(rel_bias, kt):
    qt = 2 * kt
    return pl.pallas_call(
        functools.partial(_bias_kernel, kt=kt),
        grid=(N_HEADS, 3),
        in_specs=[pl.BlockSpec(memory_space=pltpu.SMEM)],
        out_specs=pl.BlockSpec((1, kt, qt), lambda h, t: (h, t, 0)),
        out_shape=jax.ShapeDtypeStruct((N_HEADS, 3 * kt, qt), F32),
        compiler_params=_params("arbitrary", "arbitrary"),
        name="rel_bias_tiles",
    )(rel_bias)


def _norm_matmul_kernel(x_ref, g_ref, w_ref, o_ref, vt_ref, hn_ref, *, v_block):
    j = pl.program_id(1)

    @pl.when(j == 0)
    def _():
        hn_ref[...] = (_rms(x_ref[...], NORM_EPS) * g_ref[...]).astype(BF16)

    r = jnp.dot(hn_ref[...], w_ref[...], preferred_element_type=F32)
    o_ref[...] = r.astype(o_ref.dtype)

    @pl.when(j == v_block)
    def _():
        vt_ref[...] = r.T.astype(vt_ref.dtype)


def _norm_matmul(x, g, w, *, tm, tn, v_block):
    m, d = x.shape
    n = w.shape[1]
    return pl.pallas_call(
        functools.partial(_norm_matmul_kernel, v_block=v_block),
        grid=(m // tm, n // tn),
        in_specs=[
            pl.BlockSpec((tm, d), lambda i, j: (i, 0)),
            pl.BlockSpec((1, d), lambda i, j: (0, 0)),
            pl.BlockSpec((d, tn), lambda i, j: (0, j)),
        ],
        out_specs=[
            pl.BlockSpec((tm, tn), lambda i, j: (i, j)),
            pl.BlockSpec((tn, tm), lambda i, j: (0, i)),
        ],
        out_shape=[
            jax.ShapeDtypeStruct((m, n), BF16),
            jax.ShapeDtypeStruct((tn, m), BF16),
        ],
        scratch_shapes=[pltpu.VMEM((tm, d), BF16)],
        compiler_params=_params("arbitrary", "arbitrary"),
        name="norm_in_proj",
    )(x, g, w)


def _attn_kernel(q_ref, k_ref, vt_ref, bias_ref, lq1_ref, lk1_ref, lq2_ref, lk2_ref, g_ref,
                 o_ref, s_ref, m_ref, alpha_ref, l_ref, acc_ref, *, kt, lambda_init):
    qi = pl.program_id(2)
    qt = 2 * kt
    all_q, late_q = slice(0, qt), slice(kt, qt)

    q = (q_ref[0].astype(F32) * (QK_DIM ** -0.5 * LOG2E)).astype(BF16)
    lane = lax.broadcasted_iota(jnp.int32, q.shape, 1)
    zero = jnp.zeros_like(q)
    q_halves = (jnp.where(lane < QK_DIM, q, zero), jnp.where(lane >= QK_DIM, q, zero))

    m_ref[...] = jnp.full(m_ref.shape, MASK_VALUE, F32)
    l_ref[...] = jnp.zeros(l_ref.shape, F32)
    acc_ref[...] = jnp.zeros(acc_ref.shape, F32)

    def qk(t, qs):
        k_tile = k_ref[0, pl.ds(pl.multiple_of(t * kt, kt), kt), :]
        return [lax.dot_general(k_tile, q_halves[c][qs], (((1,), (1,)), ((), ())),
                                preferred_element_type=F32) for c in range(2)]

    def put_scores(scores, slot, bias, qs):
        for c, s in enumerate(scores):
            if bias is not None:
                s = s + bias
            m_prev = m_ref[1 - slot, c, :, qs]
            m_new = jnp.maximum(m_prev, jnp.max(s, axis=0, keepdims=True))
            alpha_ref[slot, c, :, qs] = jnp.exp2(m_prev - m_new)
            m_ref[slot, c, :, qs] = m_new
            s_ref[slot, c, :, qs] = s

    def accumulate(t, slot, qs):
        vt = vt_ref[:, pl.ds(pl.multiple_of(t * kt, kt), kt)]
        for c in range(2):
            alpha = alpha_ref[slot, c, :, qs]
            p = jnp.exp2(s_ref[slot, c, :, qs] - m_ref[slot, c, :, qs])
            l_ref[c, :, qs] = alpha * l_ref[c, :, qs] + jnp.sum(p, axis=0, keepdims=True)
            acc_ref[c, :, qs] = alpha * acc_ref[c, :, qs] + jnp.dot(
                vt, p.astype(BF16), preferred_element_type=F32)

    def stage(nxt, slot, bias, cur, cur_qs):
        for lo in range(0, qt, QUERY_BLOCK):
            qs = slice(lo, lo + QUERY_BLOCK)
            scores = qk(nxt, qs)
            if lo >= cur_qs.start:
                accumulate(cur, 1 - slot, qs)
            put_scores(scores, slot, None if bias is None else bias[:, qs], qs)

    put_scores(qk(2 * qi + 1, late_q), 0, bias_ref[0, 2 * kt:, late_q], late_q)
    stage(2 * qi, 1, bias_ref[0, kt:2 * kt, :], 2 * qi + 1, late_q)

    @pl.when(qi >= 1)
    def _():
        stage(2 * qi - 1, 0, bias_ref[0, :kt, :], 2 * qi, all_q)

    def far_pair(p, in_slot0):
        stage(2 * p, 1, None, in_slot0, all_q)
        stage(2 * p + 1, 0, None, 2 * p, all_q)
        return 2 * p + 1

    in_slot0 = lax.fori_loop(0, qi - 1, far_pair, jnp.maximum(2 * qi - 1, 0))

    @pl.when(qi >= 1)
    def _():
        stage(2 * qi - 2, 1, None, in_slot0, all_q)

    accumulate(jnp.maximum(2 * qi - 2, 0), 1, all_q)

    lam = (jnp.exp(jnp.sum(lq1_ref[...] * lk1_ref[...], axis=-1, keepdims=True))
           - jnp.exp(jnp.sum(lq2_ref[...] * lk2_ref[...], axis=-1, keepdims=True))
           + lambda_init)
    a = acc_ref[0] / l_ref[0] - lam * (acc_ref[1] / l_ref[1])
    a = a * lax.rsqrt(jnp.mean(a * a, axis=0, keepdims=True) + SUBLN_EPS)
    a = a * g_ref[...] * (1.0 - lambda_init)
    o_ref[0] = a.T.astype(o_ref.dtype)


def _attention(proj, vt, bias_tab, lq1, lk1, lq2, lk2, gain, *, lambda_init):
    b, s, _ = proj.shape
    kt = ATTN_K_TILE
    qt = 2 * kt
    vec = lambda n: pl.BlockSpec((1, n), lambda bi, h, qi: (0, 0))
    return pl.pallas_call(
        functools.partial(_attn_kernel, kt=kt, lambda_init=lambda_init),
        grid=(b, N_HEADS, s // qt),
        in_specs=[
            pl.BlockSpec((1, qt, HEAD_DIM), lambda bi, h, qi: (bi, qi, h)),
            pl.BlockSpec((1, s, HEAD_DIM), lambda bi, h, qi: (bi, 0, N_HEADS + h)),
            pl.BlockSpec((HEAD_DIM, s), lambda bi, h, qi: (h, bi)),
            pl.BlockSpec((1, 3 * kt, qt), lambda bi, h, qi: (h, 0, 0)),
            vec(QK_DIM), vec(QK_DIM), vec(QK_DIM), vec(QK_DIM),
            pl.BlockSpec((HEAD_DIM, 1), lambda bi, h, qi: (0, 0)),
        ],
        out_specs=pl.BlockSpec((1, qt, HEAD_DIM), lambda bi, h, qi: (bi, qi, h)),
        out_shape=jax.ShapeDtypeStruct((b, s, N_HEADS * HEAD_DIM), BF16),
        scratch_shapes=[
            pltpu.VMEM((2, 2, kt, qt), F32),
            pltpu.VMEM((2, 2, 1, qt), F32),
            pltpu.VMEM((2, 2, 1, qt), F32),
            pltpu.VMEM((2, 1, qt), F32),
            pltpu.VMEM((2, HEAD_DIM, qt), F32),
        ],
        compiler_params=_params("arbitrary", "arbitrary", "arbitrary"),
        name="diff_attention",
    )(proj, proj, vt, bias_tab, lq1, lk1, lq2, lk2, gain)


def _mix_kernel(a_ref, gb_ref, gc_ref, hc_ref, gc_halo_ref, hc_halo_ref, cw_ref, cg_ref,
                w_ref, x_ref, g_ref, o_ref, *, tiles_per_seq):
    first = pl.program_id(0) % tiles_per_seq == 0
    u = gc_ref[...].astype(F32) * hc_ref[...].astype(F32)
    halo = gc_halo_ref[...].astype(F32) * hc_halo_ref[...].astype(F32)
    halo = jnp.where(first, 0.0, halo)
    row = lax.broadcasted_iota(jnp.int32, u.shape, 0)
    u1 = jnp.where(row == 0, halo[7:8], pltpu.roll(u, 1, 0))
    u2 = jnp.where(row == 0, halo[6:7], jnp.where(row == 1, halo[7:8], pltpu.roll(u, 2, 0)))
    cw = cw_ref[...]
    c = gb_ref[...].astype(F32) * (cw[0:1] * u2 + cw[1:2] * u1 + cw[2:3] * u)
    c = (_rms(c, NORM_EPS) * cg_ref[...]).astype(BF16)

    aw = a_ref.shape[1]
    mix = jnp.dot(a_ref[...], w_ref[:aw, :], preferred_element_type=F32)
    mix = mix + jnp.dot(c, w_ref[aw:, :], preferred_element_type=F32)
    o_ref[...] = x_ref[...] + _rms(mix, NORM_EPS) * g_ref[...]


def _mix_out(a, proj, conv_w, conv_gain, w, x, g, *, tm, seq_len):
    m, d = x.shape
    aw = a.shape[1]
    cw = conv_w.shape[1]
    nblk = 3 * aw // cw
    halo_blocks = tm // SUBLANES
    row = lambda n, k=0: pl.BlockSpec((tm, n), lambda i: (i, k))
    halo = lambda k: pl.BlockSpec(
        (SUBLANES, cw), lambda i: (jnp.maximum(i * halo_blocks - 1, 0), nblk + k))
    const = lambda shape: pl.BlockSpec(shape, lambda i: (0, 0))
    return pl.pallas_call(
        functools.partial(_mix_kernel, tiles_per_seq=seq_len // tm),
        grid=(m // tm,),
        in_specs=[
            row(aw), row(cw, nblk), row(cw, nblk + 1), row(cw, nblk + 2), halo(1), halo(2),
            const((CONV_K, cw)), const((1, cw)), const(w.shape), row(d), const((1, d)),
        ],
        out_specs=row(d),
        out_shape=jax.ShapeDtypeStruct((m, d), F32),
        compiler_params=_params("arbitrary"),
        name="conv_out_proj",
    )(a, proj, proj, proj, proj, proj, conv_w, conv_gain, w, x, g)


def _ffn_kernel(x_ref, gpre_ref, wg_ref, wu_ref, wd_ref, gpost_ref, o_ref, hn_ref):
    j = pl.program_id(1)

    @pl.when(j == 0)
    def _():
        hn_ref[...] = (_rms(x_ref[...], NORM_EPS) * gpre_ref[...]).astype(BF16)
        o_ref[...] = jnp.zeros(o_ref.shape, F32)

    hn = hn_ref[...]
    gate = jnp.dot(hn, wg_ref[...], preferred_element_type=F32)
    up = jnp.dot(hn, wu_ref[...], preferred_element_type=F32)
    h = (gate * jax.nn.sigmoid(gate) * up).astype(BF16)
    o_ref[...] += jnp.dot(h, wd_ref[...], preferred_element_type=F32)

    @pl.when(j == pl.num_programs(1) - 1)
    def _():
        o_ref[...] = x_ref[...] + _rms(o_ref[...], NORM_EPS) * gpost_ref[...]


def _ffn(x, gpre, wg, wu, wd, gpost, *, tm, tf):
    m, d = x.shape
    f = wg.shape[1]
    return pl.pallas_call(
        _ffn_kernel,
        grid=(m // tm, f // tf),
        in_specs=[
            pl.BlockSpec((tm, d), lambda i, j: (i, 0)),
            pl.BlockSpec((1, d), lambda i, j: (0, 0)),
            pl.BlockSpec((d, tf), lambda i, j: (0, j)),
            pl.BlockSpec((d, tf), lambda i, j: (0, j)),
            pl.BlockSpec((tf, d), lambda i, j: (j, 0)),
            pl.BlockSpec((1, d), lambda i, j: (0, 0)),
        ],
        out_specs=pl.BlockSpec((tm, d), lambda i, j: (i, 0)),
        out_shape=jax.ShapeDtypeStruct((m, d), F32),
        scratch_shapes=[pltpu.VMEM((tm, d), BF16)],
        compiler_params=_params("arbitrary", "arbitrary"),
        name="swiglu_ffn",
    )(x, gpre, wg, wu, wd, gpost)


def kernel(x, w_in, w_out, lambda_q1, lambda_k1, lambda_q2, lambda_k2, subln_gain, conv_w, conv_norm_gain, rel_bias, w_gate, w_up, w_down, norm_mix_pre, norm_mix_post, norm_ffn_pre, norm_ffn_post):
    b, s, d = x.shape
    depth = w_in.shape[0]
    assert s % (2 * ATTN_K_TILE) == 0 and ATTN_K_TILE >= MAX_DISTANCE

    w_in, w_out, w_gate, w_up, w_down = (w.astype(BF16) for w in (w_in, w_out, w_gate, w_up, w_down))
    bias_tab = _bias_tiles(rel_bias, ATTN_K_TILE)
    row = lambda v: v.reshape(1, -1)

    xf = x.reshape(b * s, d)
    for l in range(depth):
        lambda_init = 0.8 - 0.6 * math.exp(-0.3 * l)
        proj, vt = _norm_matmul(xf, row(norm_mix_pre[l]), w_in[l], tm=1024, tn=N_HEADS * HEAD_DIM,
                                v_block=2)
        proj = proj.reshape(b, s, -1)
        a = _attention(proj, vt, bias_tab, row(lambda_q1[l]), row(lambda_k1[l]), row(lambda_q2[l]),
                       row(lambda_k2[l]), subln_gain[l].reshape(-1, 1), lambda_init=lambda_init)
        xf = _mix_out(a.reshape(b * s, -1), proj.reshape(b * s, -1), conv_w[l], row(conv_norm_gain[l]),
                      w_out[l], xf, row(norm_mix_post[l]), tm=512, seq_len=s)
        xf = _ffn(xf, row(norm_ffn_pre[l]), w_gate[l], w_up[l], w_down[l], row(norm_ffn_post[l]),
                  tm=512, tf=512)
    return xf.reshape(b, s, d)
```

```python
import functools
import math

import jax
import jax.numpy as jnp
from jax import lax
from jax.experimental import pallas as pl
from jax.experimental.pallas import tpu as pltpu

N_HEADS = 8
HEAD_DIM = 128
QK_DIM = 64
CONV_K = 3
N_BUCKETS = 32
MAX_DISTANCE = 128
NORM_EPS = 1e-6
SUBLN_EPS = 1e-5

LANES = 128
SUBLANES = 8
VMEM_LIMIT = 56 * 1024 * 1024

ROW_CHUNK = 256
ATTN_K_TILE = 512
SUM_ROWS = 16
QUERY_BLOCK = 256
MASK_VALUE = -1e30
LOG2E = math.log2(math.e)

F32 = jnp.float32
BF16 = jnp.bfloat16


def _params(*sem):
    return pltpu.CompilerParams(dimension_semantics=sem, vmem_limit_bytes=VMEM_LIMIT)


def _rms(x, eps):
    return x * lax.rsqrt(jnp.mean(x * x, axis=-1, keepdims=True) + eps)


def _row_chunks(rows):
    assert rows % ROW_CHUNK == 0
    return [slice(r, r + ROW_CHUNK) for r in range(0, rows, ROW_CHUNK)]


def _bias_kernel(rb_ref, o_ref, *, kt):
    h = pl.program_id(0)
    shape = o_ref.shape[1:]
    key = pl.program_id(1) * kt + lax.broadcasted_iota(jnp.int32, shape, 0)
    dist = kt + lax.broadcasted_iota(jnp.int32, shape, 1) - key
    n = jnp.maximum(dist, 0)
    max_exact = N_BUCKETS // 2
    nf = jnp.maximum(n, max_exact).astype(F32)
    large = max_exact + (jnp.log(nf / max_exact) / math.log(MAX_DISTANCE / max_exact)
                         * (N_BUCKETS - max_exact)).astype(jnp.int32)
    large = jnp.minimum(large, N_BUCKETS - 1)
    bucket = jnp.where(n < max_exact, n, large)
    val = jnp.zeros(shape, F32)
    for b in range(N_BUCKETS):
        val = jnp.where(bucket == b, rb_ref[b, h], val)
    val = (val - rb_ref[N_BUCKETS - 1, h]) * LOG2E
    o_ref[0] = jnp.where(dist >= 0, val, MASK_VALUE)


def _bias_tiles(rel_bias, kt):
    qt = 2 * kt
    return pl.pallas_call(
        functools.partial(_bias_kernel, kt=kt),
        grid=(N_HEADS, 3),
        in_specs=[pl.BlockSpec(memory_space=pltpu.SMEM)],
        out_specs=pl.BlockSpec((1, kt, qt), lambda h, t: (h, t, 0)),
        out_shape=jax.ShapeDtypeStruct((N_HEADS, 3 * kt, qt), F32),
        compiler_params=_params("arbitrary", "arbitrary"),
        name="rel_bias_tiles",
    )(rel_bias)


def _norm_matmul_kernel(x_ref, g_ref, w_ref, o_ref, vt_ref, hn_ref, *, v_block):
    j = pl.program_id(1)
    chunks = _row_chunks(x_ref.shape[0])

    def project(rows, hn):
        r = jnp.dot(hn, w_ref[...], preferred_element_type=F32)
        o_ref[rows, :] = r.astype(o_ref.dtype)
        return r

    @pl.when(j == 0)
    def _():
        for rows in chunks:
            hn = (_rms(x_ref[rows, :], NORM_EPS) * g_ref[...]).astype(BF16)
            hn_ref[rows, :] = hn
            project(rows, hn)

    @pl.when(j == v_block)
    def _():
        for rows in chunks:
            vt_ref[:, rows] = project(rows, hn_ref[rows, :]).T.astype(vt_ref.dtype)

    @pl.when((j != 0) & (j != v_block))
    def _():
        for rows in chunks:
            project(rows, hn_ref[rows, :])


def _norm_matmul(x, g, w, *, tm, tn, v_block):
    m, d = x.shape
    n = w.shape[1]
    return pl.pallas_call(
        functools.partial(_norm_matmul_kernel, v_block=v_block),
        grid=(m // tm, n // tn),
        in_specs=[
            pl.BlockSpec((tm, d), lambda i, j: (i, 0)),
            pl.BlockSpec((1, d), lambda i, j: (0, 0)),
            pl.BlockSpec((d, tn), lambda i, j: (0, j)),
        ],
        out_specs=[
            pl.BlockSpec((tm, tn), lambda i, j: (i, j)),
            pl.BlockSpec((tn, tm), lambda i, j: (0, i)),
        ],
        out_shape=[
            jax.ShapeDtypeStruct((m, n), BF16),
            jax.ShapeDtypeStruct((tn, m), BF16),
        ],
        scratch_shapes=[pltpu.VMEM((tm, d), BF16)],
        compiler_params=_params("arbitrary", "arbitrary"),
        name="norm_in_proj",
    )(x, g, w)


def _attn_kernel(q_ref, k_ref, vt_ref, bias_ref, lq1_ref, lk1_ref, lq2_ref, lk2_ref, g_ref,
                 o_ref, s_ref, m_ref, alpha_ref, acc_ref, *, kt, lambda_init):
    qi = pl.program_id(2)
    qt = 2 * kt
    all_q, late_q = slice(0, qt), slice(kt, qt)

    q = (q_ref[0].astype(F32) * (QK_DIM ** -0.5 * LOG2E)).astype(BF16)
    lane = lax.broadcasted_iota(jnp.int32, q.shape, 1)
    zero = jnp.zeros_like(q)
    q_halves = (jnp.where(lane < QK_DIM, q, zero), jnp.where(lane >= QK_DIM, q, zero))

    m_ref[...] = jnp.full(m_ref.shape, MASK_VALUE, F32)
    acc_ref[...] = jnp.zeros(acc_ref.shape, F32)

    def qk(t, qs):
        k_tile = k_ref[0, pl.ds(pl.multiple_of(t * kt, kt), kt), :]
        return [lax.dot_general(k_tile, q_halves[c][qs], (((1,), (1,)), ((), ())),
                                preferred_element_type=F32) for c in range(2)]

    def put_scores(scores, slot, bias, qs):
        for c, s in enumerate(scores):
            if bias is not None:
                s = s + bias
            m_prev = m_ref[1 - slot, c, :, qs]
            m_new = jnp.maximum(m_prev, jnp.max(s, axis=0, keepdims=True))
            alpha_ref[slot, c, :, qs] = jnp.exp2(m_prev - m_new)
            m_ref[slot, c, :, qs] = m_new
            s_ref[slot, c, :, qs] = s

    ones_rows = jnp.ones((SUM_ROWS, kt), BF16)

    def accumulate(t, slot, qs):
        vt = vt_ref[:, pl.ds(pl.multiple_of(t * kt, kt), kt)]
        vt = jnp.concatenate([vt, ones_rows], axis=0)
        for c in range(2):
            p = jnp.exp2(s_ref[slot, c, :, qs] - m_ref[slot, c, :, qs])
            acc_ref[c, :, qs] = alpha_ref[slot, c, :, qs] * acc_ref[c, :, qs] + jnp.dot(
                vt, p.astype(BF16), preferred_element_type=F32)

    def stage(nxt, slot, bias, cur, cur_qs):
        for lo in range(0, qt, QUERY_BLOCK):
            qs = slice(lo, lo + QUERY_BLOCK)
            scores = qk(nxt, qs)
            if lo >= cur_qs.start:
                accumulate(cur, 1 - slot, qs)
            put_scores(scores, slot, None if bias is None else bias[:, qs], qs)

    put_scores(qk(2 * qi + 1, late_q), 0, bias_ref[0, 2 * kt:, late_q], late_q)
    stage(2 * qi, 1, bias_ref[0, kt:2 * kt, :], 2 * qi + 1, late_q)

    @pl.when(qi >= 1)
    def _():
        stage(2 * qi - 1, 0, bias_ref[0, :kt, :], 2 * qi, all_q)

    def far_tiles(first, count, in_slot0):
        for t in range(0, count, 2):
            stage(first + t, 1, None, in_slot0, all_q)
            stage(first + t + 1, 0, None, first + t, all_q)
            in_slot0 = first + t + 1
        return in_slot0

    n_pairs = jnp.maximum(qi - 1, 0)
    in_slot0 = lax.fori_loop(0, n_pairs // 2, lambda p, in0: far_tiles(4 * p, 4, in0),
                             jnp.maximum(2 * qi - 1, 0))

    @pl.when(n_pairs % 2 == 1)
    def _():
        far_tiles(2 * n_pairs - 2, 2, in_slot0)

    in_slot0 = jnp.where(n_pairs % 2 == 1, 2 * n_pairs - 1, in_slot0)

    @pl.when(qi >= 1)
    def _():
        stage(2 * qi - 2, 1, None, in_slot0, all_q)

    accumulate(jnp.maximum(2 * qi - 2, 0), 1, all_q)

    lam = (jnp.exp(jnp.sum(lq1_ref[...] * lk1_ref[...], axis=-1, keepdims=True))
           - jnp.exp(jnp.sum(lq2_ref[...] * lk2_ref[...], axis=-1, keepdims=True))
           + lambda_init)
    out = [acc_ref[c, :HEAD_DIM, :] / acc_ref[c, HEAD_DIM:HEAD_DIM + 1, :] for c in range(2)]
    a = out[0] - lam * out[1]
    a = a * lax.rsqrt(jnp.mean(a * a, axis=0, keepdims=True) + SUBLN_EPS)
    a = a * g_ref[...] * (1.0 - lambda_init)
    o_ref[0] = a.T.astype(o_ref.dtype)


def _attention(proj, vt, bias_tab, lq1, lk1, lq2, lk2, gain, *, lambda_init):
    b, s, _ = proj.shape
    kt = ATTN_K_TILE
    qt = 2 * kt
    vec = lambda n: pl.BlockSpec((1, n), lambda bi, h, qi: (0, 0))
    return pl.pallas_call(
        functools.partial(_attn_kernel, kt=kt, lambda_init=lambda_init),
        grid=(b, N_HEADS, s // qt),
        in_specs=[
            pl.BlockSpec((1, qt, HEAD_DIM), lambda bi, h, qi: (bi, qi, h)),
            pl.BlockSpec((1, s, HEAD_DIM), lambda bi, h, qi: (bi, 0, N_HEADS + h)),
            pl.BlockSpec((HEAD_DIM, s), lambda bi, h, qi: (h, bi)),
            pl.BlockSpec((1, 3 * kt, qt), lambda bi, h, qi: (h, 0, 0)),
            vec(QK_DIM), vec(QK_DIM), vec(QK_DIM), vec(QK_DIM),
            pl.BlockSpec((HEAD_DIM, 1), lambda bi, h, qi: (0, 0)),
        ],
        out_specs=pl.BlockSpec((1, qt, HEAD_DIM), lambda bi, h, qi: (bi, qi, h)),
        out_shape=jax.ShapeDtypeStruct((b, s, N_HEADS * HEAD_DIM), BF16),
        scratch_shapes=[
            pltpu.VMEM((2, 2, kt, qt), F32),
            pltpu.VMEM((2, 2, 1, qt), F32),
            pltpu.VMEM((2, 2, 1, qt), F32),
            pltpu.VMEM((2, HEAD_DIM + SUM_ROWS, qt), F32),
        ],
        compiler_params=_params("arbitrary", "arbitrary", "arbitrary"),
        name="diff_attention",
    )(proj, proj, vt, bias_tab, lq1, lk1, lq2, lk2, gain)


def _mix_kernel(a_ref, gb_ref, gc_ref, hc_ref, gc_halo_ref, hc_halo_ref, cw_ref, cg_ref,
                w_ref, x_ref, g_ref, o_ref, *, tiles_per_seq):
    first = pl.program_id(0) % tiles_per_seq == 0
    u = gc_ref[...].astype(F32) * hc_ref[...].astype(F32)
    halo = gc_halo_ref[...].astype(F32) * hc_halo_ref[...].astype(F32)
    halo = jnp.where(first, 0.0, halo)
    row = lax.broadcasted_iota(jnp.int32, u.shape, 0)
    u1 = jnp.where(row == 0, halo[7:8], pltpu.roll(u, 1, 0))
    u2 = jnp.where(row == 0, halo[6:7], jnp.where(row == 1, halo[7:8], pltpu.roll(u, 2, 0)))
    cw = cw_ref[...]
    c = gb_ref[...].astype(F32) * (cw[0:1] * u2 + cw[1:2] * u1 + cw[2:3] * u)
    c = (_rms(c, NORM_EPS) * cg_ref[...]).astype(BF16)

    aw = a_ref.shape[1]
    mix = jnp.dot(a_ref[...], w_ref[:aw, :], preferred_element_type=F32)
    mix = mix + jnp.dot(c, w_ref[aw:, :], preferred_element_type=F32)
    o_ref[...] = x_ref[...] + _rms(mix, NORM_EPS) * g_ref[...]


def _mix_out(a, proj, conv_w, conv_gain, w, x, g, *, tm, seq_len):
    m, d = x.shape
    aw = a.shape[1]
    cw = conv_w.shape[1]
    nblk = 3 * aw // cw
    halo_blocks = tm // SUBLANES
    row = lambda n, k=0: pl.BlockSpec((tm, n), lambda i: (i, k))
    halo = lambda k: pl.BlockSpec(
        (SUBLANES, cw), lambda i: (jnp.maximum(i * halo_blocks - 1, 0), nblk + k))
    const = lambda shape: pl.BlockSpec(shape, lambda i: (0, 0))
    return pl.pallas_call(
        functools.partial(_mix_kernel, tiles_per_seq=seq_len // tm),
        grid=(m // tm,),
        in_specs=[
            row(aw), row(cw, nblk), row(cw, nblk + 1), row(cw, nblk + 2), halo(1), halo(2),
            const((CONV_K, cw)), const((1, cw)), const(w.shape), row(d), const((1, d)),
        ],
        out_specs=row(d),
        out_shape=jax.ShapeDtypeStruct((m, d), F32),
        compiler_params=_params("arbitrary"),
        name="conv_out_proj",
    )(a, proj, proj, proj, proj, proj, conv_w, conv_gain, w, x, g)


def _ffn_kernel(x_ref, gpre_ref, wg_ref, wu_ref, wd_ref, gpost_ref, o_ref, hn_ref):
    j = pl.program_id(1)
    last = pl.num_programs(1) - 1
    chunks = _row_chunks(x_ref.shape[0])

    def partial_out(hn):
        gate = jnp.dot(hn, wg_ref[...], preferred_element_type=F32)
        up = jnp.dot(hn, wu_ref[...], preferred_element_type=F32)
        h = (gate * jax.nn.sigmoid(gate) * up).astype(BF16)
        return jnp.dot(h, wd_ref[...], preferred_element_type=F32)

    @pl.when(j == 0)
    def _():
        for rows in chunks:
            hn = (_rms(x_ref[rows, :], NORM_EPS) * gpre_ref[...]).astype(BF16)
            hn_ref[rows, :] = hn
            o_ref[rows, :] = partial_out(hn)

    @pl.when((j > 0) & (j < last))
    def _():
        for rows in chunks:
            o_ref[rows, :] += partial_out(hn_ref[rows, :])

    @pl.when(j == last)
    def _():
        for rows in chunks:
            f = o_ref[rows, :] + partial_out(hn_ref[rows, :])
            o_ref[rows, :] = x_ref[rows, :] + _rms(f, NORM_EPS) * gpost_ref[...]


def _ffn(x, gpre, wg, wu, wd, gpost, *, tm, tf):
    m, d = x.shape
    f = wg.shape[1]
    assert f // tf >= 2
    return pl.pallas_call(
        _ffn_kernel,
        grid=(m // tm, f // tf),
        in_specs=[
            pl.BlockSpec((tm, d), lambda i, j: (i, 0)),
            pl.BlockSpec((1, d), lambda i, j: (0, 0)),
            pl.BlockSpec((d, tf), lambda i, j: (0, j)),
            pl.BlockSpec((d, tf), lambda i, j: (0, j)),
            pl.BlockSpec((tf, d), lambda i, j: (j, 0)),
            pl.BlockSpec((1, d), lambda i, j: (0, 0)),
        ],
        out_specs=pl.BlockSpec((tm, d), lambda i, j: (i, 0)),
        out_shape=jax.ShapeDtypeStruct((m, d), F32),
        scratch_shapes=[pltpu.VMEM((tm, d), BF16)],
        compiler_params=_params("arbitrary", "arbitrary"),
        name="swiglu_ffn",
    )(x, gpre, wg, wu, wd, gpost)


def kernel(x, w_in, w_out, lambda_q1, lambda_k1, lambda_q2, lambda_k2, subln_gain, conv_w, conv_norm_gain, rel_bias, w_gate, w_up, w_down, norm_mix_pre, norm_mix_post, norm_ffn_pre, norm_ffn_post):
    b, s, d = x.shape
    depth = w_in.shape[0]
    assert s % (2 * ATTN_K_TILE) == 0 and ATTN_K_TILE >= MAX_DISTANCE

    w_in, w_out, w_gate, w_up, w_down = (w.astype(BF16) for w in (w_in, w_out, w_gate, w_up, w_down))
    bias_tab = _bias_tiles(rel_bias, ATTN_K_TILE)
    row = lambda v: v.reshape(1, -1)

    xf = x.reshape(b * s, d)
    for l in range(depth):
        lambda_init = 0.8 - 0.6 * math.exp(-0.3 * l)
        proj, vt = _norm_matmul(xf, row(norm_mix_pre[l]), w_in[l], tm=1024, tn=N_HEADS * HEAD_DIM,
                                v_block=2)
        proj = proj.reshape(b, s, -1)
        a = _attention(proj, vt, bias_tab, row(lambda_q1[l]), row(lambda_k1[l]), row(lambda_q2[l]),
                       row(lambda_k2[l]), subln_gain[l].reshape(-1, 1), lambda_init=lambda_init)
        xf = _mix_out(a.reshape(b * s, -1), proj.reshape(b * s, -1), conv_w[l], row(conv_norm_gain[l]),
                      w_out[l], xf, row(norm_mix_post[l]), tm=512, seq_len=s)
        xf = _ffn(xf, row(norm_ffn_pre[l]), w_gate[l], w_up[l], w_down[l], row(norm_ffn_post[l]),
                  tm=512, tf=512)
    return xf.reshape(b, s, d)
```

```python
import functools
import math

import jax
import jax.numpy as jnp
from jax import lax
from jax.experimental import pallas as pl
from jax.experimental.pallas import tpu as pltpu

N_HEADS = 8
HEAD_DIM = 128
QK_DIM = 64
CONV_K = 3
N_BUCKETS = 32
MAX_DISTANCE = 128
NORM_EPS = 1e-6
SUBLN_EPS = 1e-5

LANES = 128
SUBLANES = 8
VMEM_LIMIT = 56 * 1024 * 1024

ROW_CHUNK = 256
ATTN_K_TILE = 512
SUM_ROWS = 16
QUERY_BLOCK = 256
MASK_VALUE = -1e30
LOG2E = math.log2(math.e)

F32 = jnp.float32
BF16 = jnp.bfloat16


def _params(*sem):
    return pltpu.CompilerParams(dimension_semantics=sem, vmem_limit_bytes=VMEM_LIMIT)


def _rms(x, eps):
    return x * lax.rsqrt(jnp.mean(x * x, axis=-1, keepdims=True) + eps)


def _row_chunks(rows):
    assert rows % ROW_CHUNK == 0
    return [slice(r, r + ROW_CHUNK) for r in range(0, rows, ROW_CHUNK)]


def _bias_kernel(rb_ref, o_ref, *, kt):
    h = pl.program_id(0)
    shape = o_ref.shape[1:]
    key = pl.program_id(1) * kt + lax.broadcasted_iota(jnp.int32, shape, 0)
    dist = kt + lax.broadcasted_iota(jnp.int32, shape, 1) - key
    n = jnp.maximum(dist, 0)
    max_exact = N_BUCKETS // 2
    nf = jnp.maximum(n, max_exact).astype(F32)
    large = max_exact + (jnp.log(nf / max_exact) / math.log(MAX_DISTANCE / max_exact)
                         * (N_BUCKETS - max_exact)).astype(jnp.int32)
    large = jnp.minimum(large, N_BUCKETS - 1)
    bucket = jnp.where(n < max_exact, n, large)
    val = jnp.zeros(shape, F32)
    for b in range(N_BUCKETS):
        val = jnp.where(bucket == b, rb_ref[b, h], val)
    val = (val - rb_ref[N_BUCKETS - 1, h]) * LOG2E
    o_ref[0] = jnp.where(dist >= 0, val, MASK_VALUE)


def _bias_tiles(rel_bias, kt):
    qt = 2 * kt
    return pl.pallas_call(
        functools.partial(_bias_kernel, kt=kt),
        grid=(N_HEADS, 3),
        in_specs=[pl.BlockSpec(memory_space=pltpu.SMEM)],
        out_specs=pl.BlockSpec((1, kt, qt), lambda h, t: (h, t, 0)),
        out_shape=jax.ShapeDtypeStruct((N_HEADS, 3 * kt, qt), F32),
        compiler_params=_params("arbitrary", "arbitrary"),
        name="rel_bias_tiles",
    )(rel_bias)


def _norm_matmul_kernel(x_ref, g_ref, w_ref, o_ref, vt_ref, hn_ref, *, v_block):
    j = pl.program_id(1)
    chunks = _row_chunks(x_ref.shape[0])

    def project(rows, hn):
        r = jnp.dot(hn, w_ref[...], preferred_element_type=F32)
        o_ref[rows, :] = r.astype(o_ref.dtype)
        return r

    @pl.when(j == 0)
    def _():
        for rows in chunks:
            hn = (_rms(x_ref[rows, :], NORM_EPS) * g_ref[...]).astype(BF16)
            hn_ref[rows, :] = hn
            project(rows, hn)

    @pl.when(j == v_block)
    def _():
        for rows in chunks:
            vt_ref[:, rows] = project(rows, hn_ref[rows, :]).T.astype(vt_ref.dtype)

    @pl.when((j != 0) & (j != v_block))
    def _():
        for rows in chunks:
            project(rows, hn_ref[rows, :])


def _norm_matmul(x, g, w, *, tm, tn, v_block):
    m, d = x.shape
    n = w.shape[1]
    return pl.pallas_call(
        functools.partial(_norm_matmul_kernel, v_block=v_block),
        grid=(m // tm, n // tn),
        in_specs=[
            pl.BlockSpec((tm, d), lambda i, j: (i, 0)),
            pl.BlockSpec((1, d), lambda i, j: (0, 0)),
            pl.BlockSpec((d, tn), lambda i, j: (0, j)),
        ],
        out_specs=[
            pl.BlockSpec((tm, tn), lambda i, j: (i, j)),
            pl.BlockSpec((tn, tm), lambda i, j: (0, i)),
        ],
        out_shape=[
            jax.ShapeDtypeStruct((m, n), BF16),
            jax.ShapeDtypeStruct((tn, m), BF16),
        ],
        scratch_shapes=[pltpu.VMEM((tm, d), BF16)],
        compiler_params=_params("arbitrary", "arbitrary"),
        name="norm_in_proj",
    )(x, g, w)


def _attn_kernel(q_ref, k_ref, vt_ref, bias_ref, lq1_ref, lk1_ref, lq2_ref, lk2_ref, g_ref,
                 o_ref, s_ref, m_ref, alpha_ref, acc_ref, *, kt, lambda_init):
    qi = pl.program_id(2)
    n_blocks = 2 * kt // QUERY_BLOCK
    all_q = range(n_blocks)
    late_q = range(n_blocks // 2, n_blocks)
    block = lambda b: slice(b * QUERY_BLOCK, (b + 1) * QUERY_BLOCK)

    q = (q_ref[0].astype(F32) * (QK_DIM ** -0.5 * LOG2E)).astype(BF16)
    lane = lax.broadcasted_iota(jnp.int32, q.shape, 1)
    zero = jnp.zeros_like(q)
    q_halves = (jnp.where(lane < QK_DIM, q, zero), jnp.where(lane >= QK_DIM, q, zero))

    m_ref[...] = jnp.full(m_ref.shape, MASK_VALUE, F32)
    acc_ref[...] = jnp.zeros(acc_ref.shape, F32)

    def qk(t, b):
        k_tile = k_ref[0, pl.ds(pl.multiple_of(t * kt, kt), kt), :]
        return [lax.dot_general(k_tile, q_halves[c][block(b)], (((1,), (1,)), ((), ())),
                                preferred_element_type=F32) for c in range(2)]

    def put_scores(scores, slot, bias_rows, b):
        for c, s in enumerate(scores):
            if bias_rows is not None:
                s = s + bias_ref[0, bias_rows, block(b)]
            m_prev = m_ref[1 - slot, c, b]
            m_new = jnp.maximum(m_prev, jnp.max(s, axis=0, keepdims=True))
            alpha_ref[slot, c, b] = jnp.exp2(m_prev - m_new)
            m_ref[slot, c, b] = m_new
            s_ref[slot, c, b] = s

    ones_rows = jnp.ones((SUM_ROWS, kt), BF16)

    def accumulate(t, slot, b):
        vt = vt_ref[:, pl.ds(pl.multiple_of(t * kt, kt), kt)]
        vt = jnp.concatenate([vt, ones_rows], axis=0)
        for c in range(2):
            p = jnp.exp2(s_ref[slot, c, b] - m_ref[slot, c, b])
            acc_ref[c, b] = alpha_ref[slot, c, b] * acc_ref[c, b] + jnp.dot(
                vt, p.astype(BF16), preferred_element_type=F32)

    def stage(nxt, slot, bias_rows, cur, cur_blocks):
        for b in all_q:
            scores = qk(nxt, b)
            if b in cur_blocks:
                accumulate(cur, 1 - slot, b)
            put_scores(scores, slot, bias_rows, b)

    for b in late_q:
        put_scores(qk(2 * qi + 1, b), 0, slice(2 * kt, 3 * kt), b)
    stage(2 * qi, 1, slice(kt, 2 * kt), 2 * qi + 1, late_q)

    @pl.when(qi >= 1)
    def _():
        stage(2 * qi - 1, 0, slice(0, kt), 2 * qi, all_q)

    def far_tiles(first, count, in_slot0):
        for t in range(0, count, 2):
            stage(first + t, 1, None, in_slot0, all_q)
            stage(first + t + 1, 0, None, first + t, all_q)
            in_slot0 = first + t + 1
        return in_slot0

    n_pairs = jnp.maximum(qi - 1, 0)
    in_slot0 = lax.fori_loop(0, n_pairs // 2, lambda p, in0: far_tiles(4 * p, 4, in0),
                             jnp.maximum(2 * qi - 1, 0))

    @pl.when(n_pairs % 2 == 1)
    def _():
        far_tiles(2 * n_pairs - 2, 2, in_slot0)

    in_slot0 = jnp.where(n_pairs % 2 == 1, 2 * n_pairs - 1, in_slot0)

    @pl.when(qi >= 1)
    def _():
        stage(2 * qi - 2, 1, None, in_slot0, all_q)

    lam = (jnp.exp(jnp.sum(lq1_ref[...] * lk1_ref[...], axis=-1, keepdims=True))
           - jnp.exp(jnp.sum(lq2_ref[...] * lk2_ref[...], axis=-1, keepdims=True))
           + lambda_init)
    for b in all_q:
        accumulate(jnp.maximum(2 * qi - 2, 0), 1, b)
        out = [acc_ref[c, b, :HEAD_DIM, :] / acc_ref[c, b, HEAD_DIM:HEAD_DIM + 1, :] for c in range(2)]
        a = out[0] - lam * out[1]
        a = a * lax.rsqrt(jnp.mean(a * a, axis=0, keepdims=True) + SUBLN_EPS)
        a = a * g_ref[...] * (1.0 - lambda_init)
        o_ref[0, block(b), :] = a.T.astype(o_ref.dtype)


def _attention(proj, vt, bias_tab, lq1, lk1, lq2, lk2, gain, *, lambda_init):
    b, s, _ = proj.shape
    kt = ATTN_K_TILE
    qt = 2 * kt
    nb = qt // QUERY_BLOCK
    vec = lambda n: pl.BlockSpec((1, n), lambda bi, h, qi: (0, 0))
    return pl.pallas_call(
        functools.partial(_attn_kernel, kt=kt, lambda_init=lambda_init),
        grid=(b, N_HEADS, s // qt),
        in_specs=[
            pl.BlockSpec((1, qt, HEAD_DIM), lambda bi, h, qi: (bi, qi, h)),
            pl.BlockSpec((1, s, HEAD_DIM), lambda bi, h, qi: (bi, 0, N_HEADS + h)),
            pl.BlockSpec((HEAD_DIM, s), lambda bi, h, qi: (h, bi)),
            pl.BlockSpec((1, 3 * kt, qt), lambda bi, h, qi: (h, 0, 0)),
            vec(QK_DIM), vec(QK_DIM), vec(QK_DIM), vec(QK_DIM),
            pl.BlockSpec((HEAD_DIM, 1), lambda bi, h, qi: (0, 0)),
        ],
        out_specs=pl.BlockSpec((1, qt, HEAD_DIM), lambda bi, h, qi: (bi, qi, h)),
        out_shape=jax.ShapeDtypeStruct((b, s, N_HEADS * HEAD_DIM), BF16),
        scratch_shapes=[
            pltpu.VMEM((2, 2, nb, kt, QUERY_BLOCK), F32),
            pltpu.VMEM((2, 2, nb, 1, QUERY_BLOCK), F32),
            pltpu.VMEM((2, 2, nb, 1, QUERY_BLOCK), F32),
            pltpu.VMEM((2, nb, HEAD_DIM + SUM_ROWS, QUERY_BLOCK), F32),
        ],
        compiler_params=_params("arbitrary", "arbitrary", "arbitrary"),
        name="diff_attention",
    )(proj, proj, vt, bias_tab, lq1, lk1, lq2, lk2, gain)


def _mix_kernel(a_ref, gb_ref, gc_ref, hc_ref, gc_halo_ref, hc_halo_ref, cw_ref, cg_ref,
                w_ref, x_ref, g_ref, o_ref, *, tiles_per_seq):
    first = pl.program_id(0) % tiles_per_seq == 0
    u = gc_ref[...].astype(F32) * hc_ref[...].astype(F32)
    halo = gc_halo_ref[...].astype(F32) * hc_halo_ref[...].astype(F32)
    halo = jnp.where(first, 0.0, halo)
    row = lax.broadcasted_iota(jnp.int32, u.shape, 0)
    u1 = jnp.where(row == 0, halo[7:8], pltpu.roll(u, 1, 0))
    u2 = jnp.where(row == 0, halo[6:7], jnp.where(row == 1, halo[7:8], pltpu.roll(u, 2, 0)))
    cw = cw_ref[...]
    c = gb_ref[...].astype(F32) * (cw[0:1] * u2 + cw[1:2] * u1 + cw[2:3] * u)
    c = (_rms(c, NORM_EPS) * cg_ref[...]).astype(BF16)

    aw = a_ref.shape[1]
    mix = jnp.dot(a_ref[...], w_ref[:aw, :], preferred_element_type=F32)
    mix = mix + jnp.dot(c, w_ref[aw:, :], preferred_element_type=F32)
    o_ref[...] = x_ref[...] + _rms(mix, NORM_EPS) * g_ref[...]


def _mix_out(a, proj, conv_w, conv_gain, w, x, g, *, tm, seq_len):
    m, d = x.shape
    aw = a.shape[1]
    cw = conv_w.shape[1]
    nblk = 3 * aw // cw
    halo_blocks = tm // SUBLANES
    row = lambda n, k=0: pl.BlockSpec((tm, n), lambda i: (i, k))
    halo = lambda k: pl.BlockSpec(
        (SUBLANES, cw), lambda i: (jnp.maximum(i * halo_blocks - 1, 0), nblk + k))
    const = lambda shape: pl.BlockSpec(shape, lambda i: (0, 0))
    return pl.pallas_call(
        functools.partial(_mix_kernel, tiles_per_seq=seq_len // tm),
        grid=(m // tm,),
        in_specs=[
            row(aw), row(cw, nblk), row(cw, nblk + 1), row(cw, nblk + 2), halo(1), halo(2),
            const((CONV_K, cw)), const((1, cw)), const(w.shape), row(d), const((1, d)),
        ],
        out_specs=row(d),
        out_shape=jax.ShapeDtypeStruct((m, d), F32),
        compiler_params=_params("arbitrary"),
        name="conv_out_proj",
    )(a, proj, proj, proj, proj, proj, conv_w, conv_gain, w, x, g)


def _ffn_kernel(x_ref, gpre_ref, wg_ref, wu_ref, wd_ref, gpost_ref, o_ref, hn_ref):
    j = pl.program_id(1)
    last = pl.num_programs(1) - 1
    chunks = _row_chunks(x_ref.shape[0])

    def partial_out(hn):
        gate = jnp.dot(hn, wg_ref[...], preferred_element_type=F32)
        up = jnp.dot(hn, wu_ref[...], preferred_element_type=F32)
        h = (gate * jax.nn.sigmoid(gate) * up).astype(BF16)
        return jnp.dot(h, wd_ref[...], preferred_element_type=F32)

    @pl.when(j == 0)
    def _():
        for rows in chunks:
            hn = (_rms(x_ref[rows, :], NORM_EPS) * gpre_ref[...]).astype(BF16)
            hn_ref[rows, :] = hn
            o_ref[rows, :] = partial_out(hn)

    @pl.when((j > 0) & (j < last))
    def _():
        o_ref[...] += partial_out(hn_ref[...])

    @pl.when(j == last)
    def _():
        for rows in chunks:
            f = o_ref[rows, :] + partial_out(hn_ref[rows, :])
            o_ref[rows, :] = x_ref[rows, :] + _rms(f, NORM_EPS) * gpost_ref[...]


def _ffn(x, gpre, wg, wu, wd, gpost, *, tm, tf):
    m, d = x.shape
    f = wg.shape[1]
    assert f // tf >= 2
    return pl.pallas_call(
        _ffn_kernel,
        grid=(m // tm, f // tf),
        in_specs=[
            pl.BlockSpec((tm, d), lambda i, j: (i, 0)),
            pl.BlockSpec((1, d), lambda i, j: (0, 0)),
            pl.BlockSpec((d, tf), lambda i, j: (0, j)),
            pl.BlockSpec((d, tf), lambda i, j: (0, j)),
            pl.BlockSpec((tf, d), lambda i, j: (j, 0)),
            pl.BlockSpec((1, d), lambda i, j: (0, 0)),
        ],
        out_specs=pl.BlockSpec((tm, d), lambda i, j: (i, 0)),
        out_shape=jax.ShapeDtypeStruct((m, d), F32),
        scratch_shapes=[pltpu.VMEM((tm, d), BF16)],
        compiler_params=_params("arbitrary", "arbitrary"),
        name="swiglu_ffn",
    )(x, gpre, wg, wu, wd, gpost)


def kernel(x, w_in, w_out, lambda_q1, lambda_k1, lambda_q2, lambda_k2, subln_gain, conv_w, conv_norm_gain, rel_bias, w_gate, w_up, w_down, norm_mix_pre, norm_mix_post, norm_ffn_pre, norm_ffn_post):
    b, s, d = x.shape
    depth = w_in.shape[0]
    assert s % (2 * ATTN_K_TILE) == 0 and ATTN_K_TILE >= MAX_DISTANCE

    w_in, w_out, w_gate, w_up, w_down = (w.astype(BF16) for w in (w_in, w_out, w_gate, w_up, w_down))
    bias_tab = _bias_tiles(rel_bias, ATTN_K_TILE)
    row = lambda v: v.reshape(1, -1)

    xf = x.reshape(b * s, d)
    for l in range(depth):
        lambda_init = 0.8 - 0.6 * math.exp(-0.3 * l)
        proj, vt = _norm_matmul(xf, row(norm_mix_pre[l]), w_in[l], tm=1024, tn=N_HEADS * HEAD_DIM,
                                v_block=2)
        proj = proj.reshape(b, s, -1)
        a = _attention(proj, vt, bias_tab, row(lambda_q1[l]), row(lambda_k1[l]), row(lambda_q2[l]),
                       row(lambda_k2[l]), subln_gain[l].reshape(-1, 1), lambda_init=lambda_init)
        xf = _mix_out(a.reshape(b * s, -1), proj.reshape(b * s, -1), conv_w[l], row(conv_norm_gain[l]),
                      w_out[l], xf, row(norm_mix_post[l]), tm=512, seq_len=s)
        xf = _ffn(xf, row(norm_ffn_pre[l]), w_gate[l], w_up[l], w_down[l], row(norm_ffn_post[l]),
                  tm=512, tf=512)
    return xf.reshape(b, s, d)
```

```python
import functools
import math

import jax
import jax.numpy as jnp
from jax import lax
from jax.experimental import pallas as pl
from jax.experimental.pallas import tpu as pltpu

N_HEADS = 8
HEAD_DIM = 128
QK_DIM = 64
CONV_K = 3
N_BUCKETS = 32
MAX_DISTANCE = 128
NORM_EPS = 1e-6
SUBLN_EPS = 1e-5

LANES = 128
SUBLANES = 8
VMEM_LIMIT = 56 * 1024 * 1024

ROW_CHUNK = 256
ATTN_K_TILE = 512
SUM_ROWS = 16
QUERY_BLOCK = 256
MASK_VALUE = -1e30
LOG2E = math.log2(math.e)

F32 = jnp.float32
BF16 = jnp.bfloat16


def _params(*sem):
    return pltpu.CompilerParams(dimension_semantics=sem, vmem_limit_bytes=VMEM_LIMIT)


def _rms(x, eps):
    return x * lax.rsqrt(jnp.mean(x * x, axis=-1, keepdims=True) + eps)


def _row_chunks(rows):
    assert rows % ROW_CHUNK == 0
    return [slice(r, r + ROW_CHUNK) for r in range(0, rows, ROW_CHUNK)]


def _bias_kernel(rb_ref, o_ref, *, kt):
    h = pl.program_id(0)
    shape = o_ref.shape[1:]
    key = pl.program_id(1) * kt + lax.broadcasted_iota(jnp.int32, shape, 0)
    dist = kt + lax.broadcasted_iota(jnp.int32, shape, 1) - key
    n = jnp.maximum(dist, 0)
    max_exact = N_BUCKETS // 2
    nf = jnp.maximum(n, max_exact).astype(F32)
    large = max_exact + (jnp.log(nf / max_exact) / math.log(MAX_DISTANCE / max_exact)
                         * (N_BUCKETS - max_exact)).astype(jnp.int32)
    large = jnp.minimum(large, N_BUCKETS - 1)
    bucket = jnp.where(n < max_exact, n, large)
    val = jnp.zeros(shape, F32)
    for b in range(N_BUCKETS):
        val = jnp.where(bucket == b, rb_ref[b, h], val)
    val = (val - rb_ref[N_BUCKETS - 1, h]) * LOG2E
    o_ref[0] = jnp.where(dist >= 0, val, MASK_VALUE)


def _bias_tiles(rel_bias, kt):
    qt = 2 * kt
    return pl.pallas_call(
        functools.partial(_bias_kernel, kt=kt),
        grid=(N_HEADS, 3),
        in_specs=[pl.BlockSpec(memory_space=pltpu.SMEM)],
        out_specs=pl.BlockSpec((1, kt, qt), lambda h, t: (h, t, 0)),
        out_shape=jax.ShapeDtypeStruct((N_HEADS, 3 * kt, qt), F32),
        compiler_params=_params("arbitrary", "arbitrary"),
        name="rel_bias_tiles",
    )(rel_bias)


def _norm_matmul_kernel(x_ref, g_ref, w_ref, o_ref, vt_ref, hn_ref, *, v_block):
    j = pl.program_id(1)
    chunks = _row_chunks(x_ref.shape[0])

    def project(rows, hn):
        r = jnp.dot(hn, w_ref[...], preferred_element_type=F32)
        o_ref[rows, :] = r.astype(o_ref.dtype)
        return r

    @pl.when(j == 0)
    def _():
        for rows in chunks:
            hn = (_rms(x_ref[rows, :], NORM_EPS) * g_ref[...]).astype(BF16)
            hn_ref[rows, :] = hn
            project(rows, hn)

    @pl.when(j == v_block)
    def _():
        for rows in chunks:
            vt_ref[:, rows] = project(rows, hn_ref[rows, :]).T.astype(vt_ref.dtype)

    @pl.when((j != 0) & (j != v_block))
    def _():
        for rows in chunks:
            project(rows, hn_ref[rows, :])


def _norm_matmul(x, g, w, *, tm, tn, v_block):
    m, d = x.shape
    n = w.shape[1]
    return pl.pallas_call(
        functools.partial(_norm_matmul_kernel, v_block=v_block),
        grid=(m // tm, n // tn),
        in_specs=[
            pl.BlockSpec((tm, d), lambda i, j: (i, 0)),
            pl.BlockSpec((1, d), lambda i, j: (0, 0)),
            pl.BlockSpec((d, tn), lambda i, j: (0, j)),
        ],
        out_specs=[
            pl.BlockSpec((tm, tn), lambda i, j: (i, j)),
            pl.BlockSpec((tn, tm), lambda i, j: (0, i)),
        ],
        out_shape=[
            jax.ShapeDtypeStruct((m, n), BF16),
            jax.ShapeDtypeStruct((tn, m), BF16),
        ],
        scratch_shapes=[pltpu.VMEM((tm, d), BF16)],
        compiler_params=_params("arbitrary", "arbitrary"),
        name="norm_in_proj",
    )(x, g, w)


def _attn_kernel(q_ref, k_ref, vt_ref, bias_ref, lq1_ref, lk1_ref, lq2_ref, lk2_ref, g_ref,
                 o_ref, s_ref, m_ref, alpha_ref, acc_ref, *, kt, lambda_init):
    qi = pl.program_id(2)
    n_blocks = 2 * kt // QUERY_BLOCK
    all_q = range(n_blocks)
    late_q = range(n_blocks // 2, n_blocks)
    block = lambda b: slice(b * QUERY_BLOCK, (b + 1) * QUERY_BLOCK)

    q = (q_ref[0].astype(F32) * (QK_DIM ** -0.5 * LOG2E)).astype(BF16)
    lane = lax.broadcasted_iota(jnp.int32, q.shape, 1)
    zero = jnp.zeros_like(q)
    q_halves = (jnp.where(lane < QK_DIM, q, zero), jnp.where(lane >= QK_DIM, q, zero))

    m_ref[...] = jnp.full(m_ref.shape, MASK_VALUE, F32)
    acc_ref[...] = jnp.zeros(acc_ref.shape, F32)

    def qk(t, b):
        k_tile = k_ref[0, pl.ds(pl.multiple_of(t * kt, kt), kt), :]
        return [lax.dot_general(k_tile, q_halves[c][block(b)], (((1,), (1,)), ((), ())),
                                preferred_element_type=F32) for c in range(2)]

    def put_scores(scores, slot, bias_rows, b):
        for c, s in enumerate(scores):
            if bias_rows is not None:
                s = s + bias_ref[0, bias_rows, block(b)]
            m_prev = m_ref[1 - slot, c, b]
            m_new = jnp.maximum(m_prev, jnp.max(s, axis=0, keepdims=True))
            alpha_ref[slot, c, b] = jnp.exp2(m_prev - m_new)
            m_ref[slot, c, b] = m_new
            s_ref[slot, c, b] = s

    ones_rows = jnp.ones((SUM_ROWS, kt), BF16)

    def accumulate(t, slot, b):
        vt = vt_ref[:, pl.ds(pl.multiple_of(t * kt, kt), kt)]
        vt = jnp.concatenate([vt, ones_rows], axis=0)
        for c in range(2):
            p = jnp.exp2(s_ref[slot, c, b] - m_ref[slot, c, b])
            acc_ref[c, b] = alpha_ref[slot, c, b] * acc_ref[c, b] + jnp.dot(
                vt, p.astype(BF16), preferred_element_type=F32)

    def stage(nxt, slot, bias_rows, cur, cur_blocks):
        for b in all_q:
            scores = qk(nxt, b)
            if b in cur_blocks:
                accumulate(cur, 1 - slot, b)
            put_scores(scores, slot, bias_rows, b)

    for b in late_q:
        put_scores(qk(2 * qi + 1, b), 0, slice(2 * kt, 3 * kt), b)
    stage(2 * qi, 1, slice(kt, 2 * kt), 2 * qi + 1, late_q)

    @pl.when(qi >= 1)
    def _():
        stage(2 * qi - 1, 0, slice(0, kt), 2 * qi, all_q)

    def far_tiles(first, count, in_slot0):
        for t in range(0, count, 2):
            stage(first + t, 1, None, in_slot0, all_q)
            stage(first + t + 1, 0, None, first + t, all_q)
            in_slot0 = first + t + 1
        return in_slot0

    n_pairs = jnp.maximum(qi - 1, 0)
    in_slot0 = lax.fori_loop(0, n_pairs // 2, lambda p, in0: far_tiles(4 * p, 4, in0),
                             jnp.maximum(2 * qi - 1, 0))

    @pl.when(n_pairs % 2 == 1)
    def _():
        far_tiles(2 * n_pairs - 2, 2, in_slot0)

    in_slot0 = jnp.where(n_pairs % 2 == 1, 2 * n_pairs - 1, in_slot0)

    @pl.when(qi >= 1)
    def _():
        stage(2 * qi - 2, 1, None, in_slot0, all_q)

    lam = (jnp.exp(jnp.sum(lq1_ref[...] * lk1_ref[...], axis=-1, keepdims=True))
           - jnp.exp(jnp.sum(lq2_ref[...] * lk2_ref[...], axis=-1, keepdims=True))
           + lambda_init)
    for b in all_q:
        accumulate(jnp.maximum(2 * qi - 2, 0), 1, b)
        out = [acc_ref[c, b, :HEAD_DIM, :] / acc_ref[c, b, HEAD_DIM:HEAD_DIM + 1, :] for c in range(2)]
        a = out[0] - lam * out[1]
        a = a * lax.rsqrt(jnp.mean(a * a, axis=0, keepdims=True) + SUBLN_EPS)
        a = a * g_ref[...] * (1.0 - lambda_init)
        o_ref[0, block(b), :] = a.T.astype(o_ref.dtype)


def _attention(proj, vt, bias_tab, lq1, lk1, lq2, lk2, gain, *, lambda_init):
    b, s, _ = proj.shape
    kt = ATTN_K_TILE
    qt = 2 * kt
    nb = qt // QUERY_BLOCK
    vec = lambda n: pl.BlockSpec((1, n), lambda bi, h, qi: (0, 0))
    return pl.pallas_call(
        functools.partial(_attn_kernel, kt=kt, lambda_init=lambda_init),
        grid=(b, N_HEADS, s // qt),
        in_specs=[
            pl.BlockSpec((1, qt, HEAD_DIM), lambda bi, h, qi: (bi, qi, h)),
            pl.BlockSpec((1, s, HEAD_DIM), lambda bi, h, qi: (bi, 0, N_HEADS + h)),
            pl.BlockSpec((HEAD_DIM, s), lambda bi, h, qi: (h, bi)),
            pl.BlockSpec((1, 3 * kt, qt), lambda bi, h, qi: (h, 0, 0)),
            vec(QK_DIM), vec(QK_DIM), vec(QK_DIM), vec(QK_DIM),
            pl.BlockSpec((HEAD_DIM, 1), lambda bi, h, qi: (0, 0)),
        ],
        out_specs=pl.BlockSpec((1, qt, HEAD_DIM), lambda bi, h, qi: (bi, qi, h)),
        out_shape=jax.ShapeDtypeStruct((b, s, N_HEADS * HEAD_DIM), BF16),
        scratch_shapes=[
            pltpu.VMEM((2, 2, nb, kt, QUERY_BLOCK), F32),
            pltpu.VMEM((2, 2, nb, 1, QUERY_BLOCK), F32),
            pltpu.VMEM((2, 2, nb, 1, QUERY_BLOCK), F32),
            pltpu.VMEM((2, nb, HEAD_DIM + SUM_ROWS, QUERY_BLOCK), F32),
        ],
        compiler_params=_params("arbitrary", "arbitrary", "arbitrary"),
        name="diff_attention",
    )(proj, proj, vt, bias_tab, lq1, lk1, lq2, lk2, gain)


def _mix_kernel(a_ref, gb_ref, gc_ref, hc_ref, gc_halo_ref, hc_halo_ref, cw_ref, cg_ref,
                w_ref, x_ref, g_ref, o_ref, *, tiles_per_seq):
    first = pl.program_id(0) % tiles_per_seq == 0
    u = gc_ref[...].astype(F32) * hc_ref[...].astype(F32)
    halo = gc_halo_ref[...].astype(F32) * hc_halo_ref[...].astype(F32)
    halo = jnp.where(first, 0.0, halo)
    row = lax.broadcasted_iota(jnp.int32, u.shape, 0)
    u1 = jnp.where(row == 0, halo[7:8], pltpu.roll(u, 1, 0))
    u2 = jnp.where(row == 0, halo[6:7], jnp.where(row == 1, halo[7:8], pltpu.roll(u, 2, 0)))
    cw = cw_ref[...]
    c = gb_ref[...].astype(F32) * (cw[0:1] * u2 + cw[1:2] * u1 + cw[2:3] * u)
    c = (_rms(c, NORM_EPS) * cg_ref[...]).astype(BF16)

    aw = a_ref.shape[1]
    mix = jnp.dot(a_ref[...], w_ref[:aw, :], preferred_element_type=F32)
    mix = mix + jnp.dot(c, w_ref[aw:, :], preferred_element_type=F32)
    o_ref[...] = x_ref[...] + _rms(mix, NORM_EPS) * g_ref[...]


def _mix_out(a, proj, conv_w, conv_gain, w, x, g, *, tm, seq_len):
    m, d = x.shape
    aw = a.shape[1]
    cw = conv_w.shape[1]
    nblk = 3 * aw // cw
    halo_blocks = tm // SUBLANES
    row = lambda n, k=0: pl.BlockSpec((tm, n), lambda i: (i, k))
    halo = lambda k: pl.BlockSpec(
        (SUBLANES, cw), lambda i: (jnp.maximum(i * halo_blocks - 1, 0), nblk + k))
    const = lambda shape: pl.BlockSpec(shape, lambda i: (0, 0))
    return pl.pallas_call(
        functools.partial(_mix_kernel, tiles_per_seq=seq_len // tm),
        grid=(m // tm,),
        in_specs=[
            row(aw), row(cw, nblk), row(cw, nblk + 1), row(cw, nblk + 2), halo(1), halo(2),
            const((CONV_K, cw)), const((1, cw)), const(w.shape), row(d), const((1, d)),
        ],
        out_specs=row(d),
        out_shape=jax.ShapeDtypeStruct((m, d), F32),
        compiler_params=_params("arbitrary"),
        name="conv_out_proj",
    )(a, proj, proj, proj, proj, proj, conv_w, conv_gain, w, x, g)


def _ffn_kernel(x_ref, gpre_ref, wg_ref, wu_ref, wd_ref, gpost_ref, o_ref, hn_ref):
    j = pl.program_id(1)
    last = pl.num_programs(1) - 1
    chunks = _row_chunks(x_ref.shape[0])

    def partial_out(hn):
        gate = jnp.dot(hn, wg_ref[...], preferred_element_type=F32)
        up = jnp.dot(hn, wu_ref[...], preferred_element_type=F32)
        h = (gate * jax.nn.sigmoid(gate) * up).astype(BF16)
        return jnp.dot(h, wd_ref[...], preferred_element_type=F32)

    @pl.when(j == 0)
    def _():
        for rows in chunks:
            hn = (_rms(x_ref[rows, :], NORM_EPS) * gpre_ref[...]).astype(BF16)
            hn_ref[rows, :] = hn
            o_ref[rows, :] = partial_out(hn)

    @pl.when((j > 0) & (j < last))
    def _():
        o_ref[...] += partial_out(hn_ref[...])

    @pl.when(j == last)
    def _():
        for rows in chunks:
            f = o_ref[rows, :] + partial_out(hn_ref[rows, :])
            o_ref[rows, :] = x_ref[rows, :] + _rms(f, NORM_EPS) * gpost_ref[...]


def _ffn(x, gpre, wg, wu, wd, gpost, *, tm, tf):
    m, d = x.shape
    f = wg.shape[1]
    assert f // tf >= 2
    return pl.pallas_call(
        _ffn_kernel,
        grid=(m // tm, f // tf),
        in_specs=[
            pl.BlockSpec((tm, d), lambda i, j: (i, 0)),
            pl.BlockSpec((1, d), lambda i, j: (0, 0)),
            pl.BlockSpec((d, tf), lambda i, j: (0, j)),
            pl.BlockSpec((d, tf), lambda i, j: (0, j)),
            pl.BlockSpec((tf, d), lambda i, j: (j, 0)),
            pl.BlockSpec((1, d), lambda i, j: (0, 0)),
        ],
        out_specs=pl.BlockSpec((tm, d), lambda i, j: (i, 0)),
        out_shape=jax.ShapeDtypeStruct((m, d), F32),
        scratch_shapes=[pltpu.VMEM((tm, d), BF16)],
        compiler_params=_params("arbitrary", "arbitrary"),
        name="swiglu_ffn",
    )(x, gpre, wg, wu, wd, gpost)


def kernel(x, w_in, w_out, lambda_q1, lambda_k1, lambda_q2, lambda_k2, subln_gain, conv_w, conv_norm_gain, rel_bias, w_gate, w_up, w_down, norm_mix_pre, norm_mix_post, norm_ffn_pre, norm_ffn_post):
    b, s, d = x.shape
    depth = w_in.shape[0]
    assert s % (2 * ATTN_K_TILE) == 0 and ATTN_K_TILE >= MAX_DISTANCE

    w_in, w_out, w_gate, w_up, w_down = (w.astype(BF16) for w in (w_in, w_out, w_gate, w_up, w_down))
    bias_tab = _bias_tiles(rel_bias, ATTN_K_TILE)
    row = lambda v: v.reshape(1, -1)

    xf = x.reshape(b * s, d)
    for l in range(depth):
        lambda_init = 0.8 - 0.6 * math.exp(-0.3 * l)
        proj, vt = _norm_matmul(xf, row(norm_mix_pre[l]), w_in[l], tm=1024, tn=N_HEADS * HEAD_DIM,
                                v_block=2)
        proj = proj.reshape(b, s, -1)
        a = _attention(proj, vt, bias_tab, row(lambda_q1[l]), row(lambda_k1[l]), row(lambda_q2[l]),
                       row(lambda_k2[l]), subln_gain[l].reshape(-1, 1), lambda_init=lambda_init)
        xf = _mix_out(a.reshape(b * s, -1), proj.reshape(b * s, -1), conv_w[l], row(conv_norm_gain[l]),
                      w_out[l], xf, row(norm_mix_post[l]), tm=512, seq_len=s)
        xf = _ffn(xf, row(norm_ffn_pre[l]), w_gate[l], w_up[l], w_down[l], row(norm_ffn_post[l]),
                  tm=1024, tf=512)
    return xf.reshape(b, s, d)
```

```python
import functools
import math

import jax
import jax.numpy as jnp
from jax import lax
from jax.experimental import pallas as pl
from jax.experimental.pallas import tpu as pltpu

N_HEADS = 8
HEAD_DIM = 128
QK_DIM = 64
CONV_K = 3
N_BUCKETS = 32
MAX_DISTANCE = 128
NORM_EPS = 1e-6
SUBLN_EPS = 1e-5

LANES = 128
SUBLANES = 8
VMEM_LIMIT = 56 * 1024 * 1024

ROW_CHUNK = 256
ATTN_K_TILE = 512
SUM_ROWS = 16
QUERY_BLOCK = 256
MASK_VALUE = -1e30
LOG2E = math.log2(math.e)

F32 = jnp.float32
BF16 = jnp.bfloat16


def _params(*sem):
    return pltpu.CompilerParams(dimension_semantics=sem, vmem_limit_bytes=VMEM_LIMIT)


def _rms(x, eps):
    return x * lax.rsqrt(jnp.mean(x * x, axis=-1, keepdims=True) + eps)


def _row_chunks(rows):
    assert rows % ROW_CHUNK == 0
    return [slice(r, r + ROW_CHUNK) for r in range(0, rows, ROW_CHUNK)]


def _bias_kernel(rb_ref, o_ref, *, kt):
    h = pl.program_id(0)
    shape = (kt, o_ref.shape[2] * QUERY_BLOCK)
    key = pl.program_id(1) * kt + lax.broadcasted_iota(jnp.int32, shape, 0)
    dist = kt + lax.broadcasted_iota(jnp.int32, shape, 1) - key
    n = jnp.maximum(dist, 0)
    max_exact = N_BUCKETS // 2
    nf = jnp.maximum(n, max_exact).astype(F32)
    large = max_exact + (jnp.log(nf / max_exact) / math.log(MAX_DISTANCE / max_exact)
                         * (N_BUCKETS - max_exact)).astype(jnp.int32)
    large = jnp.minimum(large, N_BUCKETS - 1)
    bucket = jnp.where(n < max_exact, n, large)
    val = jnp.zeros(shape, F32)
    for b in range(N_BUCKETS):
        val = jnp.where(bucket == b, rb_ref[b, h], val)
    val = (val - rb_ref[N_BUCKETS - 1, h]) * LOG2E
    val = jnp.where(dist >= 0, val, MASK_VALUE)
    for b in range(shape[1] // QUERY_BLOCK):
        o_ref[0, 0, b] = val[:, b * QUERY_BLOCK:(b + 1) * QUERY_BLOCK]


def _bias_tiles(rel_bias, kt):
    nb = 2 * kt // QUERY_BLOCK
    return pl.pallas_call(
        functools.partial(_bias_kernel, kt=kt),
        grid=(N_HEADS, 3),
        in_specs=[pl.BlockSpec(memory_space=pltpu.SMEM)],
        out_specs=pl.BlockSpec((1, 1, nb, kt, QUERY_BLOCK), lambda h, t: (h, t, 0, 0, 0)),
        out_shape=jax.ShapeDtypeStruct((N_HEADS, 3, nb, kt, QUERY_BLOCK), F32),
        compiler_params=_params("arbitrary", "arbitrary"),
        name="rel_bias_tiles",
    )(rel_bias)


def _norm_matmul_kernel(x_ref, g_ref, w_ref, o_ref, vt_ref, hn_ref, *, v_block):
    j = pl.program_id(1)
    chunks = _row_chunks(x_ref.shape[0])

    def project(rows, hn):
        r = jnp.dot(hn, w_ref[...], preferred_element_type=F32)
        o_ref[rows, :] = r.astype(o_ref.dtype)
        return r

    @pl.when(j == 0)
    def _():
        for rows in chunks:
            hn = (_rms(x_ref[rows, :], NORM_EPS) * g_ref[...]).astype(BF16)
            hn_ref[rows, :] = hn
            project(rows, hn)

    @pl.when(j == v_block)
    def _():
        for rows in chunks:
            vt_ref[:, rows] = project(rows, hn_ref[rows, :]).T.astype(vt_ref.dtype)

    @pl.when((j != 0) & (j != v_block))
    def _():
        for rows in chunks:
            project(rows, hn_ref[rows, :])


def _norm_matmul(x, g, w, *, tm, tn, v_block):
    m, d = x.shape
    n = w.shape[1]
    return pl.pallas_call(
        functools.partial(_norm_matmul_kernel, v_block=v_block),
        grid=(m // tm, n // tn),
        in_specs=[
            pl.BlockSpec((tm, d), lambda i, j: (i, 0)),
            pl.BlockSpec((1, d), lambda i, j: (0, 0)),
            pl.BlockSpec((d, tn), lambda i, j: (0, j)),
        ],
        out_specs=[
            pl.BlockSpec((tm, tn), lambda i, j: (i, j)),
            pl.BlockSpec((tn, tm), lambda i, j: (0, i)),
        ],
        out_shape=[
            jax.ShapeDtypeStruct((m, n), BF16),
            jax.ShapeDtypeStruct((tn, m), BF16),
        ],
        scratch_shapes=[pltpu.VMEM((tm, d), BF16)],
        compiler_params=_params("arbitrary", "arbitrary"),
        name="norm_in_proj",
    )(x, g, w)


def _attn_kernel(q_ref, k_ref, vt_ref, bias_ref, lq1_ref, lk1_ref, lq2_ref, lk2_ref, g_ref,
                 o_ref, s_ref, m_ref, alpha_ref, acc_ref, *, kt, lambda_init):
    qi = pl.program_id(2)
    n_blocks = 2 * kt // QUERY_BLOCK
    all_q = range(n_blocks)
    late_q = range(n_blocks // 2, n_blocks)
    block = lambda b: slice(b * QUERY_BLOCK, (b + 1) * QUERY_BLOCK)

    q = (q_ref[0].astype(F32) * (QK_DIM ** -0.5 * LOG2E)).astype(BF16)
    lane = lax.broadcasted_iota(jnp.int32, q.shape, 1)
    zero = jnp.zeros_like(q)
    q_halves = (jnp.where(lane < QK_DIM, q, zero), jnp.where(lane >= QK_DIM, q, zero))

    m_ref[...] = jnp.full(m_ref.shape, MASK_VALUE, F32)
    acc_ref[...] = jnp.zeros(acc_ref.shape, F32)

    def qk(t, b):
        k_tile = k_ref[0, pl.ds(pl.multiple_of(t * kt, kt), kt), :]
        return [lax.dot_general(k_tile, q_halves[c][block(b)], (((1,), (1,)), ((), ())),
                                preferred_element_type=F32) for c in range(2)]

    def put_scores(scores, slot, bias_tile, b):
        for c, s in enumerate(scores):
            if bias_tile is not None:
                s = s + bias_ref[0, bias_tile, b]
            m_prev = m_ref[1 - slot, c, b]
            m_new = jnp.maximum(m_prev, jnp.max(s, axis=0, keepdims=True))
            alpha_ref[slot, c, b] = jnp.exp2(m_prev - m_new)
            m_ref[slot, c, b] = m_new
            for h in range(QUERY_BLOCK // LANES):
                s_ref[slot, c, b, h] = s[:, h * LANES:(h + 1) * LANES]

    ones_rows = jnp.ones((SUM_ROWS, kt), BF16)

    def accumulate(t, slot, b):
        vt = vt_ref[:, pl.ds(pl.multiple_of(t * kt, kt), kt)]
        vt = jnp.concatenate([vt, ones_rows], axis=0)
        for c in range(2):
            s = jnp.concatenate([s_ref[slot, c, b, h] for h in range(QUERY_BLOCK // LANES)], axis=1)
            p = jnp.exp2(s - m_ref[slot, c, b])
            acc_ref[c, b] = alpha_ref[slot, c, b] * acc_ref[c, b] + jnp.dot(
                vt, p.astype(BF16), preferred_element_type=F32)

    def stage(nxt, slot, bias_tile, cur, cur_blocks):
        for b in all_q:
            scores = qk(nxt, b)
            if b in cur_blocks:
                accumulate(cur, 1 - slot, b)
            put_scores(scores, slot, bias_tile, b)

    for b in late_q:
        put_scores(qk(2 * qi + 1, b), 0, 2, b)
    stage(2 * qi, 1, 1, 2 * qi + 1, late_q)

    @pl.when(qi >= 1)
    def _():
        stage(2 * qi - 1, 0, 0, 2 * qi, all_q)
        stage(2 * qi - 2, 1, None, 2 * qi - 1, all_q)

    def far_tiles(first, count, in_slot1):
        for t in range(0, count, 2):
            stage(first + t, 0, None, in_slot1, all_q)
            stage(first + t + 1, 1, None, first + t, all_q)
            in_slot1 = first + t + 1
        return in_slot1

    n_pairs = jnp.maximum(qi - 1, 0)
    in_slot1 = lax.fori_loop(0, n_pairs // 2, lambda p, in1: far_tiles(4 * p, 4, in1),
                             jnp.maximum(2 * qi - 2, 0))

    @pl.when(n_pairs % 2 == 1)
    def _():
        far_tiles(2 * n_pairs - 2, 2, in_slot1)

    in_slot1 = jnp.where(n_pairs % 2 == 1, 2 * n_pairs - 1, in_slot1)

    lam = (jnp.exp(jnp.sum(lq1_ref[...] * lk1_ref[...], axis=-1, keepdims=True))
           - jnp.exp(jnp.sum(lq2_ref[...] * lk2_ref[...], axis=-1, keepdims=True))
           + lambda_init)
    for b in all_q:
        accumulate(in_slot1, 1, b)
        out = [acc_ref[c, b, :HEAD_DIM, :] / acc_ref[c, b, HEAD_DIM:HEAD_DIM + 1, :] for c in range(2)]
        a = out[0] - lam * out[1]
        a = a * lax.rsqrt(jnp.mean(a * a, axis=0, keepdims=True) + SUBLN_EPS)
        a = a * g_ref[...] * (1.0 - lambda_init)
        o_ref[0, block(b), :] = a.T.astype(o_ref.dtype)


def _attention(proj, vt, bias_tab, lq1, lk1, lq2, lk2, gain, *, lambda_init):
    b, s, _ = proj.shape
    kt = ATTN_K_TILE
    qt = 2 * kt
    nb = qt // QUERY_BLOCK
    vec = lambda n: pl.BlockSpec((1, n), lambda bi, h, qi: (0, 0))
    return pl.pallas_call(
        functools.partial(_attn_kernel, kt=kt, lambda_init=lambda_init),
        grid=(b, N_HEADS, s // qt),
        in_specs=[
            pl.BlockSpec((1, qt, HEAD_DIM), lambda bi, h, qi: (bi, qi, h)),
            pl.BlockSpec((1, s, HEAD_DIM), lambda bi, h, qi: (bi, 0, N_HEADS + h)),
            pl.BlockSpec((HEAD_DIM, s), lambda bi, h, qi: (h, bi)),
            pl.BlockSpec((1, 3, nb, kt, QUERY_BLOCK), lambda bi, h, qi: (h, 0, 0, 0, 0)),
            vec(QK_DIM), vec(QK_DIM), vec(QK_DIM), vec(QK_DIM),
            pl.BlockSpec((HEAD_DIM, 1), lambda bi, h, qi: (0, 0)),
        ],
        out_specs=pl.BlockSpec((1, qt, HEAD_DIM), lambda bi, h, qi: (bi, qi, h)),
        out_shape=jax.ShapeDtypeStruct((b, s, N_HEADS * HEAD_DIM), BF16),
        scratch_shapes=[
            pltpu.VMEM((2, 2, nb, QUERY_BLOCK // LANES, kt, LANES), F32),
            pltpu.VMEM((2, 2, nb, 1, QUERY_BLOCK), F32),
            pltpu.VMEM((2, 2, nb, 1, QUERY_BLOCK), F32),
            pltpu.VMEM((2, nb, HEAD_DIM + SUM_ROWS, QUERY_BLOCK), F32),
        ],
        compiler_params=_params("arbitrary", "arbitrary", "arbitrary"),
        name="diff_attention",
    )(proj, proj, vt, bias_tab, lq1, lk1, lq2, lk2, gain)


def _mix_kernel(a_ref, gb_ref, gc_ref, hc_ref, gc_halo_ref, hc_halo_ref, cw_ref, cg_ref,
                w_ref, x_ref, g_ref, o_ref, *, tiles_per_seq):
    first = pl.program_id(0) % tiles_per_seq == 0
    u = gc_ref[...].astype(F32) * hc_ref[...].astype(F32)
    halo = gc_halo_ref[...].astype(F32) * hc_halo_ref[...].astype(F32)
    halo = jnp.where(first, 0.0, halo)
    row = lax.broadcasted_iota(jnp.int32, u.shape, 0)
    u1 = jnp.where(row == 0, halo[7:8], pltpu.roll(u, 1, 0))
    u2 = jnp.where(row == 0, halo[6:7], jnp.where(row == 1, halo[7:8], pltpu.roll(u, 2, 0)))
    cw = cw_ref[...]
    c = gb_ref[...].astype(F32) * (cw[0:1] * u2 + cw[1:2] * u1 + cw[2:3] * u)
    c = (_rms(c, NORM_EPS) * cg_ref[...]).astype(BF16)

    aw = a_ref.shape[1]
    mix = jnp.dot(a_ref[...], w_ref[:aw, :], preferred_element_type=F32)
    mix = mix + jnp.dot(c, w_ref[aw:, :], preferred_element_type=F32)
    o_ref[...] = x_ref[...] + _rms(mix, NORM_EPS) * g_ref[...]


def _mix_out(a, proj, conv_w, conv_gain, w, x, g, *, tm, seq_len):
    m, d = x.shape
    aw = a.shape[1]
    cw = conv_w.shape[1]
    nblk = 3 * aw // cw
    halo_blocks = tm // SUBLANES
    row = lambda n, k=0: pl.BlockSpec((tm, n), lambda i: (i, k))
    halo = lambda k: pl.BlockSpec(
        (SUBLANES, cw), lambda i: (jnp.maximum(i * halo_blocks - 1, 0), nblk + k))
    const = lambda shape: pl.BlockSpec(shape, lambda i: (0, 0))
    return pl.pallas_call(
        functools.partial(_mix_kernel, tiles_per_seq=seq_len // tm),
        grid=(m // tm,),
        in_specs=[
            row(aw), row(cw, nblk), row(cw, nblk + 1), row(cw, nblk + 2), halo(1), halo(2),
            const((CONV_K, cw)), const((1, cw)), const(w.shape), row(d), const((1, d)),
        ],
        out_specs=row(d),
        out_shape=jax.ShapeDtypeStruct((m, d), F32),
        compiler_params=_params("arbitrary"),
        name="conv_out_proj",
    )(a, proj, proj, proj, proj, proj, conv_w, conv_gain, w, x, g)


def _ffn_kernel(x_ref, gpre_ref, wg_ref, wu_ref, wd_ref, gpost_ref, o_ref, hn_ref):
    j = pl.program_id(1)
    last = pl.num_programs(1) - 1
    chunks = _row_chunks(x_ref.shape[0])

    def partial_out(hn):
        gate = jnp.dot(hn, wg_ref[...], preferred_element_type=F32)
        up = jnp.dot(hn, wu_ref[...], preferred_element_type=F32)
        h = (gate * jax.nn.sigmoid(gate) * up).astype(BF16)
        return jnp.dot(h, wd_ref[...], preferred_element_type=F32)

    @pl.when(j == 0)
    def _():
        for rows in chunks:
            hn = (_rms(x_ref[rows, :], NORM_EPS) * gpre_ref[...]).astype(BF16)
            hn_ref[rows, :] = hn
            o_ref[rows, :] = partial_out(hn)

    @pl.when((j > 0) & (j < last))
    def _():
        o_ref[...] += partial_out(hn_ref[...])

    @pl.when(j == last)
    def _():
        for rows in chunks:
            f = o_ref[rows, :] + partial_out(hn_ref[rows, :])
            o_ref[rows, :] = x_ref[rows, :] + _rms(f, NORM_EPS) * gpost_ref[...]


def _ffn(x, gpre, wg, wu, wd, gpost, *, tm, tf):
    m, d = x.shape
    f = wg.shape[1]
    assert f // tf >= 2
    return pl.pallas_call(
        _ffn_kernel,
        grid=(m // tm, f // tf),
        in_specs=[
            pl.BlockSpec((tm, d), lambda i, j: (i, 0)),
            pl.BlockSpec((1, d), lambda i, j: (0, 0)),
            pl.BlockSpec((d, tf), lambda i, j: (0, j)),
            pl.BlockSpec((d, tf), lambda i, j: (0, j)),
            pl.BlockSpec((tf, d), lambda i, j: (j, 0)),
            pl.BlockSpec((1, d), lambda i, j: (0, 0)),
        ],
        out_specs=pl.BlockSpec((tm, d), lambda i, j: (i, 0)),
        out_shape=jax.ShapeDtypeStruct((m, d), F32),
        scratch_shapes=[pltpu.VMEM((tm, d), BF16)],
        compiler_params=_params("arbitrary", "arbitrary"),
        name="swiglu_ffn",
    )(x, gpre, wg, wu, wd, gpost)


def kernel(x, w_in, w_out, lambda_q1, lambda_k1, lambda_q2, lambda_k2, subln_gain, conv_w, conv_norm_gain, rel_bias, w_gate, w_up, w_down, norm_mix_pre, norm_mix_post, norm_ffn_pre, norm_ffn_post):
    b, s, d = x.shape
    depth = w_in.shape[0]
    assert s % (2 * ATTN_K_TILE) == 0 and ATTN_K_TILE >= MAX_DISTANCE

    w_in, w_out, w_gate, w_up, w_down = (w.astype(BF16) for w in (w_in, w_out, w_gate, w_up, w_down))
    bias_tab = _bias_tiles(rel_bias, ATTN_K_TILE)
    row = lambda v: v.reshape(1, -1)

    xf = x.reshape(b * s, d)
    for l in range(depth):
        lambda_init = 0.8 - 0.6 * math.exp(-0.3 * l)
        proj, vt = _norm_matmul(xf, row(norm_mix_pre[l]), w_in[l], tm=1024, tn=N_HEADS * HEAD_DIM,
                                v_block=2)
        proj = proj.reshape(b, s, -1)
        a = _attention(proj, vt, bias_tab, row(lambda_q1[l]), row(lambda_k1[l]), row(lambda_q2[l]),
                       row(lambda_k2[l]), subln_gain[l].reshape(-1, 1), lambda_init=lambda_init)
        xf = _mix_out(a.reshape(b * s, -1), proj.reshape(b * s, -1), conv_w[l], row(conv_norm_gain[l]),
                      w_out[l], xf, row(norm_mix_post[l]), tm=512, seq_len=s)
        xf = _ffn(xf, row(norm_ffn_pre[l]), w_gate[l], w_up[l], w_down[l], row(norm_ffn_post[l]),
                  tm=1024, tf=512)
    return xf.reshape(b, s, d)
```

```python
import functools
import math

import jax
import jax.numpy as jnp
from jax import lax
from jax.experimental import pallas as pl
from jax.experimental.pallas import tpu as pltpu

N_HEADS = 8
HEAD_DIM = 128
QK_DIM = 64
CONV_K = 3
N_BUCKETS = 32
MAX_DISTANCE = 128
NORM_EPS = 1e-6
SUBLN_EPS = 1e-5

LANES = 128
SUBLANES = 8
VMEM_LIMIT = 56 * 1024 * 1024

ROW_CHUNK = 256
ATTN_K_TILE = 512
SUM_ROWS = 16
QUERY_BLOCK = 256
MASK_VALUE = -1e30
LOG2E = math.log2(math.e)

F32 = jnp.float32
BF16 = jnp.bfloat16


def _params(*sem):
    return pltpu.CompilerParams(dimension_semantics=sem, vmem_limit_bytes=VMEM_LIMIT)


def _rms(x, eps):
    return x * lax.rsqrt(jnp.mean(x * x, axis=-1, keepdims=True) + eps)


def _row_chunks(rows):
    assert rows % ROW_CHUNK == 0
    return [slice(r, r + ROW_CHUNK) for r in range(0, rows, ROW_CHUNK)]


def _bias_kernel(rb_ref, o_ref, *, kt):
    h = pl.program_id(0)
    shape = (kt, o_ref.shape[2] * QUERY_BLOCK)
    key = pl.program_id(1) * kt + lax.broadcasted_iota(jnp.int32, shape, 0)
    dist = kt + lax.broadcasted_iota(jnp.int32, shape, 1) - key
    n = jnp.maximum(dist, 0)
    max_exact = N_BUCKETS // 2
    nf = jnp.maximum(n, max_exact).astype(F32)
    large = max_exact + (jnp.log(nf / max_exact) / math.log(MAX_DISTANCE / max_exact)
                         * (N_BUCKETS - max_exact)).astype(jnp.int32)
    large = jnp.minimum(large, N_BUCKETS - 1)
    bucket = jnp.where(n < max_exact, n, large)
    val = jnp.zeros(shape, F32)
    for b in range(N_BUCKETS):
        val = jnp.where(bucket == b, rb_ref[b, h], val)
    val = (val - rb_ref[N_BUCKETS - 1, h]) * LOG2E
    val = jnp.where(dist >= 0, val, MASK_VALUE)
    for b in range(shape[1] // QUERY_BLOCK):
        o_ref[0, 0, b] = val[:, b * QUERY_BLOCK:(b + 1) * QUERY_BLOCK]


def _bias_tiles(rel_bias, kt):
    nb = 2 * kt // QUERY_BLOCK
    return pl.pallas_call(
        functools.partial(_bias_kernel, kt=kt),
        grid=(N_HEADS, 3),
        in_specs=[pl.BlockSpec(memory_space=pltpu.SMEM)],
        out_specs=pl.BlockSpec((1, 1, nb, kt, QUERY_BLOCK), lambda h, t: (h, t, 0, 0, 0)),
        out_shape=jax.ShapeDtypeStruct((N_HEADS, 3, nb, kt, QUERY_BLOCK), F32),
        compiler_params=_params("arbitrary", "arbitrary"),
        name="rel_bias_tiles",
    )(rel_bias)


Q_BLOCK, K_BLOCK, V_BLOCK, GATE_B_BLOCK, GATE_C_BLOCK, H_BLOCK = range(6)


def _in_proj_kernel(x_ref, g_ref, w_ref, cw_ref, cg_ref, qk_ref, vt_ref, c_ref,
                    hn_ref, gb_ref, gc_ref, tail_ref, *, tiles_per_seq):
    i, j = pl.program_id(0), pl.program_id(1)
    chunks = _row_chunks(x_ref.shape[0])
    project = lambda hn: jnp.dot(hn, w_ref[...], preferred_element_type=F32)

    @pl.when(j == Q_BLOCK)
    def _():
        for rows in chunks:
            hn = (_rms(x_ref[rows, :], NORM_EPS) * g_ref[...]).astype(BF16)
            hn_ref[rows, :] = hn
            qk_ref[rows, :] = project(hn).astype(qk_ref.dtype)

    @pl.when(j == K_BLOCK)
    def _():
        for rows in chunks:
            qk_ref[rows, :] = project(hn_ref[rows, :]).astype(qk_ref.dtype)

    @pl.when(j == V_BLOCK)
    def _():
        for rows in chunks:
            vt_ref[:, rows] = project(hn_ref[rows, :]).T.astype(vt_ref.dtype)

    @pl.when(j == GATE_B_BLOCK)
    def _():
        for rows in chunks:
            gb_ref[rows, :] = project(hn_ref[rows, :])

    @pl.when(j == GATE_C_BLOCK)
    def _():
        for rows in chunks:
            gc_ref[rows, :] = project(hn_ref[rows, :])

    @pl.when(j == H_BLOCK)
    def _():
        above = jnp.where(i % tiles_per_seq == 0, 0.0, tail_ref[...])
        cw = cw_ref[...]
        row = lax.broadcasted_iota(jnp.int32, above.shape, 0)
        for rows in chunks:
            u = gc_ref[rows, :] * project(hn_ref[rows, :])
            u1, u2 = pltpu.roll(u, 1, 0), pltpu.roll(u, 2, 0)
            head1 = jnp.where(row == 0, above[7:8], u1[:SUBLANES])
            head2 = jnp.where(row == 0, above[6:7], jnp.where(row == 1, above[7:8], u2[:SUBLANES]))
            u1 = jnp.concatenate([head1, u1[SUBLANES:]], axis=0)
            u2 = jnp.concatenate([head2, u2[SUBLANES:]], axis=0)
            c = gb_ref[rows, :] * (cw[0:1] * u2 + cw[1:2] * u1 + cw[2:3] * u)
            c_ref[rows, :] = (_rms(c, NORM_EPS) * cg_ref[...]).astype(c_ref.dtype)
            above = u[-SUBLANES:]
        tail_ref[...] = above


def _in_proj(x, g, w, conv_w, conv_gain, *, tm, seq_len):
    m, d = x.shape
    tn = w.shape[1] // 6
    assert conv_w.shape[1] == tn and seq_len % tm == 0
    const = lambda shape: pl.BlockSpec(shape, lambda i, j: (0, 0))
    return pl.pallas_call(
        functools.partial(_in_proj_kernel, tiles_per_seq=seq_len // tm),
        grid=(m // tm, 6),
        in_specs=[
            pl.BlockSpec((tm, d), lambda i, j: (i, 0)),
            const((1, d)),
            pl.BlockSpec((d, tn), lambda i, j: (0, j)),
            const((CONV_K, tn)), const((1, tn)),
        ],
        out_specs=[
            pl.BlockSpec((tm, tn), lambda i, j: (i, jnp.minimum(j, K_BLOCK))),
            pl.BlockSpec((tn, tm), lambda i, j: (0, i)),
            pl.BlockSpec((tm, tn), lambda i, j: (i, 0)),
        ],
        out_shape=[
            jax.ShapeDtypeStruct((m, 2 * tn), BF16),
            jax.ShapeDtypeStruct((tn, m), BF16),
            jax.ShapeDtypeStruct((m, tn), BF16),
        ],
        scratch_shapes=[
            pltpu.VMEM((tm, d), BF16),
            pltpu.VMEM((tm, tn), F32),
            pltpu.VMEM((tm, tn), F32),
            pltpu.VMEM((SUBLANES, tn), F32),
        ],
        compiler_params=_params("arbitrary", "arbitrary"),
        name="norm_in_proj_conv",
    )(x, g, w, conv_w, conv_gain)


def _attn_kernel(q_ref, k_ref, vt_ref, bias_ref, lq1_ref, lk1_ref, lq2_ref, lk2_ref, g_ref,
                 o_ref, s_ref, m_ref, alpha_ref, acc_ref, *, kt, lambda_init):
    qi = pl.program_id(2)
    n_blocks = 2 * kt // QUERY_BLOCK
    all_q = range(n_blocks)
    late_q = range(n_blocks // 2, n_blocks)
    block = lambda b: slice(b * QUERY_BLOCK, (b + 1) * QUERY_BLOCK)

    q = (q_ref[0].astype(F32) * (QK_DIM ** -0.5 * LOG2E)).astype(BF16)
    lane = lax.broadcasted_iota(jnp.int32, q.shape, 1)
    zero = jnp.zeros_like(q)
    q_halves = (jnp.where(lane < QK_DIM, q, zero), jnp.where(lane >= QK_DIM, q, zero))

    m_ref[...] = jnp.full(m_ref.shape, MASK_VALUE, F32)
    acc_ref[...] = jnp.zeros(acc_ref.shape, F32)

    def qk(t, b):
        k_tile = k_ref[0, pl.ds(pl.multiple_of(t * kt, kt), kt), :]
        return [lax.dot_general(k_tile, q_halves[c][block(b)], (((1,), (1,)), ((), ())),
                                preferred_element_type=F32) for c in range(2)]

    def put_scores(scores, slot, bias_tile, b):
        for c, s in enumerate(scores):
            if bias_tile is not None:
                s = s + bias_ref[0, bias_tile, b]
            m_prev = m_ref[1 - slot, c, b]
            m_new = jnp.maximum(m_prev, jnp.max(s, axis=0, keepdims=True))
            alpha_ref[slot, c, b] = jnp.exp2(m_prev - m_new)
            m_ref[slot, c, b] = m_new
            for h in range(QUERY_BLOCK // LANES):
                s_ref[slot, c, b, h] = s[:, h * LANES:(h + 1) * LANES]

    ones_rows = jnp.ones((SUM_ROWS, kt), BF16)

    def accumulate(t, slot, b):
        vt = vt_ref[:, pl.ds(pl.multiple_of(t * kt, kt), kt)]
        vt = jnp.concatenate([vt, ones_rows], axis=0)
        for c in range(2):
            s = jnp.concatenate([s_ref[slot, c, b, h] for h in range(QUERY_BLOCK // LANES)], axis=1)
            p = jnp.exp2(s - m_ref[slot, c, b])
            acc_ref[c, b] = alpha_ref[slot, c, b] * acc_ref[c, b] + jnp.dot(
                vt, p.astype(BF16), preferred_element_type=F32)

    def stage(nxt, slot, bias_tile, cur, cur_blocks):
        for b in all_q:
            scores = qk(nxt, b)
            if b in cur_blocks:
                accumulate(cur, 1 - slot, b)
            put_scores(scores, slot, bias_tile, b)

    for b in late_q:
        put_scores(qk(2 * qi + 1, b), 0, 2, b)
    stage(2 * qi, 1, 1, 2 * qi + 1, late_q)

    @pl.when(qi >= 1)
    def _():
        stage(2 * qi - 1, 0, 0, 2 * qi, all_q)
        stage(2 * qi - 2, 1, None, 2 * qi - 1, all_q)

    def far_tiles(first, count, in_slot1):
        for t in range(0, count, 2):
            stage(first + t, 0, None, in_slot1, all_q)
            stage(first + t + 1, 1, None, first + t, all_q)
            in_slot1 = first + t + 1
        return in_slot1

    n_pairs = jnp.maximum(qi - 1, 0)
    in_slot1 = lax.fori_loop(0, n_pairs // 2, lambda p, in1: far_tiles(4 * p, 4, in1),
                             jnp.maximum(2 * qi - 2, 0))

    @pl.when(n_pairs % 2 == 1)
    def _():
        far_tiles(2 * n_pairs - 2, 2, in_slot1)

    in_slot1 = jnp.where(n_pairs % 2 == 1, 2 * n_pairs - 1, in_slot1)

    lam = (jnp.exp(jnp.sum(lq1_ref[...] * lk1_ref[...], axis=-1, keepdims=True))
           - jnp.exp(jnp.sum(lq2_ref[...] * lk2_ref[...], axis=-1, keepdims=True))
           + lambda_init)
    for b in all_q:
        accumulate(in_slot1, 1, b)
        out = [acc_ref[c, b, :HEAD_DIM, :] / acc_ref[c, b, HEAD_DIM:HEAD_DIM + 1, :] for c in range(2)]
        a = out[0] - lam * out[1]
        a = a * lax.rsqrt(jnp.mean(a * a, axis=0, keepdims=True) + SUBLN_EPS)
        a = a * g_ref[...] * (1.0 - lambda_init)
        o_ref[0, block(b), :] = a.T.astype(o_ref.dtype)


def _attention(qk, vt, bias_tab, lq1, lk1, lq2, lk2, gain, *, lambda_init):
    b, s, _ = qk.shape
    kt = ATTN_K_TILE
    qt = 2 * kt
    nb = qt // QUERY_BLOCK
    vec = lambda n: pl.BlockSpec((1, n), lambda bi, h, qi: (0, 0))
    return pl.pallas_call(
        functools.partial(_attn_kernel, kt=kt, lambda_init=lambda_init),
        grid=(b, N_HEADS, s // qt),
        in_specs=[
            pl.BlockSpec((1, qt, HEAD_DIM), lambda bi, h, qi: (bi, qi, h)),
            pl.BlockSpec((1, s, HEAD_DIM), lambda bi, h, qi: (bi, 0, N_HEADS + h)),
            pl.BlockSpec((HEAD_DIM, s), lambda bi, h, qi: (h, bi)),
            pl.BlockSpec((1, 3, nb, kt, QUERY_BLOCK), lambda bi, h, qi: (h, 0, 0, 0, 0)),
            vec(QK_DIM), vec(QK_DIM), vec(QK_DIM), vec(QK_DIM),
            pl.BlockSpec((HEAD_DIM, 1), lambda bi, h, qi: (0, 0)),
        ],
        out_specs=pl.BlockSpec((1, qt, HEAD_DIM), lambda bi, h, qi: (bi, qi, h)),
        out_shape=jax.ShapeDtypeStruct((b, s, N_HEADS * HEAD_DIM), BF16),
        scratch_shapes=[
            pltpu.VMEM((2, 2, nb, QUERY_BLOCK // LANES, kt, LANES), F32),
            pltpu.VMEM((2, 2, nb, 1, QUERY_BLOCK), F32),
            pltpu.VMEM((2, 2, nb, 1, QUERY_BLOCK), F32),
            pltpu.VMEM((2, nb, HEAD_DIM + SUM_ROWS, QUERY_BLOCK), F32),
        ],
        compiler_params=_params("arbitrary", "arbitrary", "arbitrary"),
        name="diff_attention",
    )(qk, qk, vt, bias_tab, lq1, lk1, lq2, lk2, gain)


def _out_proj_kernel(a_ref, c_ref, w_ref, x_ref, g_ref, o_ref):
    aw = a_ref.shape[1]
    for rows in _row_chunks(a_ref.shape[0]):
        mix = jnp.dot(a_ref[rows, :], w_ref[:aw, :], preferred_element_type=F32)
        mix = mix + jnp.dot(c_ref[rows, :], w_ref[aw:, :], preferred_element_type=F32)
        o_ref[rows, :] = x_ref[rows, :] + _rms(mix, NORM_EPS) * g_ref[...]


def _out_proj(a, c, w, x, g, *, tm):
    m, d = x.shape
    row = lambda n: pl.BlockSpec((tm, n), lambda i: (i, 0))
    const = lambda shape: pl.BlockSpec(shape, lambda i: (0, 0))
    return pl.pallas_call(
        _out_proj_kernel,
        grid=(m // tm,),
        in_specs=[row(a.shape[1]), row(c.shape[1]), const(w.shape), row(d), const((1, d))],
        out_specs=row(d),
        out_shape=jax.ShapeDtypeStruct((m, d), F32),
        compiler_params=_params("arbitrary"),
        name="out_proj",
    )(a, c, w, x, g)


def _ffn_kernel(x_ref, gpre_ref, wg_ref, wu_ref, wd_ref, gpost_ref, o_ref, hn_ref):
    j = pl.program_id(1)
    last = pl.num_programs(1) - 1
    chunks = _row_chunks(x_ref.shape[0])

    def partial_out(hn):
        gate = jnp.dot(hn, wg_ref[...], preferred_element_type=F32)
        up = jnp.dot(hn, wu_ref[...], preferred_element_type=F32)
        h = (gate * jax.nn.sigmoid(gate) * up).astype(BF16)
        return jnp.dot(h, wd_ref[...], preferred_element_type=F32)

    @pl.when(j == 0)
    def _():
        for rows in chunks:
            hn = (_rms(x_ref[rows, :], NORM_EPS) * gpre_ref[...]).astype(BF16)
            hn_ref[rows, :] = hn
            o_ref[rows, :] = partial_out(hn)

    @pl.when((j > 0) & (j < last))
    def _():
        o_ref[...] += partial_out(hn_ref[...])

    @pl.when(j == last)
    def _():
        for rows in chunks:
            f = o_ref[rows, :] + partial_out(hn_ref[rows, :])
            o_ref[rows, :] = x_ref[rows, :] + _rms(f, NORM_EPS) * gpost_ref[...]


def _ffn(x, gpre, wg, wu, wd, gpost, *, tm, tf):
    m, d = x.shape
    f = wg.shape[1]
    assert f // tf >= 2
    return pl.pallas_call(
        _ffn_kernel,
        grid=(m // tm, f // tf),
        in_specs=[
            pl.BlockSpec((tm, d), lambda i, j: (i, 0)),
            pl.BlockSpec((1, d), lambda i, j: (0, 0)),
            pl.BlockSpec((d, tf), lambda i, j: (0, j)),
            pl.BlockSpec((d, tf), lambda i, j: (0, j)),
            pl.BlockSpec((tf, d), lambda i, j: (j, 0)),
            pl.BlockSpec((1, d), lambda i, j: (0, 0)),
        ],
        out_specs=pl.BlockSpec((tm, d), lambda i, j: (i, 0)),
        out_shape=jax.ShapeDtypeStruct((m, d), F32),
        scratch_shapes=[pltpu.VMEM((tm, d), BF16)],
        compiler_params=_params("arbitrary", "arbitrary"),
        name="swiglu_ffn",
    )(x, gpre, wg, wu, wd, gpost)


def kernel(x, w_in, w_out, lambda_q1, lambda_k1, lambda_q2, lambda_k2, subln_gain, conv_w, conv_norm_gain, rel_bias, w_gate, w_up, w_down, norm_mix_pre, norm_mix_post, norm_ffn_pre, norm_ffn_post):
    b, s, d = x.shape
    depth = w_in.shape[0]
    assert s % (2 * ATTN_K_TILE) == 0 and ATTN_K_TILE >= MAX_DISTANCE

    w_in, w_out, w_gate, w_up, w_down = (w.astype(BF16) for w in (w_in, w_out, w_gate, w_up, w_down))
    bias_tab = _bias_tiles(rel_bias, ATTN_K_TILE)
    row = lambda v: v.reshape(1, -1)

    xf = x.reshape(b * s, d)
    for l in range(depth):
        lambda_init = 0.8 - 0.6 * math.exp(-0.3 * l)
        qk, vt, c = _in_proj(xf, row(norm_mix_pre[l]), w_in[l], conv_w[l], row(conv_norm_gain[l]),
                             tm=1024, seq_len=s)
        a = _attention(qk.reshape(b, s, -1), vt, bias_tab, row(lambda_q1[l]), row(lambda_k1[l]),
                       row(lambda_q2[l]), row(lambda_k2[l]), subln_gain[l].reshape(-1, 1),
                       lambda_init=lambda_init)
        xf = _out_proj(a.reshape(b * s, -1), c, w_out[l], xf, row(norm_mix_post[l]), tm=512)
        xf = _ffn(xf, row(norm_ffn_pre[l]), w_gate[l], w_up[l], w_down[l], row(norm_ffn_post[l]),
                  tm=1024, tf=512)
    return xf.reshape(b, s, d)
```

```python
import functools
import math

import jax
import jax.numpy as jnp
from jax import lax
from jax.experimental import pallas as pl
from jax.experimental.pallas import tpu as pltpu

N_HEADS = 8
HEAD_DIM = 128
QK_DIM = 64
CONV_K = 3
N_BUCKETS = 32
MAX_DISTANCE = 128
NORM_EPS = 1e-6
SUBLN_EPS = 1e-5

LANES = 128
SUBLANES = 8
VMEM_LIMIT = 56 * 1024 * 1024

ROW_CHUNK = 256
ATTN_K_TILE = 512
SUM_ROWS = 16
QUERY_BLOCK = 256
MASK_VALUE = -1e30
LOG2E = math.log2(math.e)

F32 = jnp.float32
BF16 = jnp.bfloat16


def _params(*sem):
    return pltpu.CompilerParams(dimension_semantics=sem, vmem_limit_bytes=VMEM_LIMIT)


def _rms(x, eps):
    return x * lax.rsqrt(jnp.mean(x * x, axis=-1, keepdims=True) + eps)


def _row_chunks(rows):
    assert rows % ROW_CHUNK == 0
    return [slice(r, r + ROW_CHUNK) for r in range(0, rows, ROW_CHUNK)]


def _bias_window(rb_ref, h, key0, query0, shape):
    dist = (query0 - key0 + lax.broadcasted_iota(jnp.int32, shape, 1)
            - lax.broadcasted_iota(jnp.int32, shape, 0))
    n = jnp.maximum(dist, 0)
    max_exact = N_BUCKETS // 2
    nf = jnp.maximum(n, max_exact).astype(F32)
    large = max_exact + (jnp.log(nf / max_exact) / math.log(MAX_DISTANCE / max_exact)
                         * (N_BUCKETS - max_exact)).astype(jnp.int32)
    large = jnp.minimum(large, N_BUCKETS - 1)
    bucket = jnp.where(n < max_exact, n, large)
    val = jnp.zeros(shape, F32)
    for b in range(N_BUCKETS):
        val = jnp.where(bucket == b, rb_ref[b, h], val)
    val = (val - rb_ref[N_BUCKETS - 1, h]) * LOG2E
    return jnp.where(dist >= 0, val, MASK_VALUE)


def _bias_kernel(rb_ref, o_ref, *, kt):
    h = pl.program_id(0)
    _, n_tiles, n_blocks, _, _ = o_ref.shape
    qt = n_blocks * QUERY_BLOCK
    assert MAX_DISTANCE <= LANES
    for t in range(n_tiles):
        for k0 in range(0, kt, LANES):
            key0 = (t - 1) * kt + k0
            lo, hi = max(key0, 0), min(key0 + 2 * LANES, qt)
            parts = []
            if lo > 0:
                parts.append(jnp.full((LANES, lo), MASK_VALUE, F32))
            if hi > lo:
                parts.append(_bias_window(rb_ref, h, key0, lo, (LANES, hi - lo)))
            if max(hi, lo) < qt:
                parts.append(jnp.zeros((LANES, qt - max(hi, lo)), F32))
            strip = jnp.concatenate(parts, axis=1) if len(parts) > 1 else parts[0]
            for b in range(n_blocks):
                o_ref[0, t, b, k0:k0 + LANES, :] = strip[:, b * QUERY_BLOCK:(b + 1) * QUERY_BLOCK]


def _bias_tiles(rel_bias, kt):
    nb = 2 * kt // QUERY_BLOCK
    shape = (N_HEADS, 3, nb, kt, QUERY_BLOCK)
    return pl.pallas_call(
        functools.partial(_bias_kernel, kt=kt),
        grid=(N_HEADS,),
        in_specs=[pl.BlockSpec(memory_space=pltpu.SMEM)],
        out_specs=pl.BlockSpec((1,) + shape[1:], lambda h: (h, 0, 0, 0, 0)),
        out_shape=jax.ShapeDtypeStruct(shape, F32),
        compiler_params=_params("arbitrary"),
        name="rel_bias_tiles",
    )(rel_bias)


Q_BLOCK, K_BLOCK, V_BLOCK, GATE_B_BLOCK, GATE_C_BLOCK, H_BLOCK = range(6)


def _in_proj_kernel(x_ref, g_ref, w_ref, cw_ref, cg_ref, qk_ref, vt_ref, c_ref,
                    hn_ref, gb_ref, gc_ref, tail_ref, *, tiles_per_seq):
    i, j = pl.program_id(0), pl.program_id(1)
    chunks = _row_chunks(x_ref.shape[0])
    project = lambda hn: jnp.dot(hn, w_ref[...], preferred_element_type=F32)

    @pl.when(j == Q_BLOCK)
    def _():
        for rows in chunks:
            hn = (_rms(x_ref[rows, :], NORM_EPS) * g_ref[...]).astype(BF16)
            hn_ref[rows, :] = hn
            qk_ref[rows, :] = project(hn).astype(qk_ref.dtype)

    @pl.when(j == K_BLOCK)
    def _():
        for rows in chunks:
            qk_ref[rows, :] = project(hn_ref[rows, :]).astype(qk_ref.dtype)

    @pl.when(j == V_BLOCK)
    def _():
        for rows in chunks:
            vt_ref[:, rows] = project(hn_ref[rows, :]).T.astype(vt_ref.dtype)

    @pl.when(j == GATE_B_BLOCK)
    def _():
        for rows in chunks:
            gb_ref[rows, :] = project(hn_ref[rows, :])

    @pl.when(j == GATE_C_BLOCK)
    def _():
        for rows in chunks:
            gc_ref[rows, :] = project(hn_ref[rows, :])

    @pl.when(j == H_BLOCK)
    def _():
        above = jnp.where(i % tiles_per_seq == 0, 0.0, tail_ref[...])
        cw = cw_ref[...]
        row = lax.broadcasted_iota(jnp.int32, above.shape, 0)
        for rows in chunks:
            u = gc_ref[rows, :] * project(hn_ref[rows, :])
            u1, u2 = pltpu.roll(u, 1, 0), pltpu.roll(u, 2, 0)
            head1 = jnp.where(row == 0, above[7:8], u1[:SUBLANES])
            head2 = jnp.where(row == 0, above[6:7], jnp.where(row == 1, above[7:8], u2[:SUBLANES]))
            u1 = jnp.concatenate([head1, u1[SUBLANES:]], axis=0)
            u2 = jnp.concatenate([head2, u2[SUBLANES:]], axis=0)
            c = gb_ref[rows, :] * (cw[0:1] * u2 + cw[1:2] * u1 + cw[2:3] * u)
            c_ref[rows, :] = (_rms(c, NORM_EPS) * cg_ref[...]).astype(c_ref.dtype)
            above = u[-SUBLANES:]
        tail_ref[...] = above


def _in_proj(x, g, w, conv_w, conv_gain, *, tm, seq_len):
    m, d = x.shape
    tn = w.shape[1] // 6
    assert conv_w.shape[1] == tn and seq_len % tm == 0
    const = lambda shape: pl.BlockSpec(shape, lambda i, j: (0, 0))
    return pl.pallas_call(
        functools.partial(_in_proj_kernel, tiles_per_seq=seq_len // tm),
        grid=(m // tm, 6),
        in_specs=[
            pl.BlockSpec((tm, d), lambda i, j: (i, 0)),
            const((1, d)),
            pl.BlockSpec((d, tn), lambda i, j: (0, j)),
            const((CONV_K, tn)), const((1, tn)),
        ],
        out_specs=[
            pl.BlockSpec((tm, tn), lambda i, j: (i, jnp.minimum(j, K_BLOCK))),
            pl.BlockSpec((tn, tm), lambda i, j: (0, i)),
            pl.BlockSpec((tm, tn), lambda i, j: (i, 0)),
        ],
        out_shape=[
            jax.ShapeDtypeStruct((m, 2 * tn), BF16),
            jax.ShapeDtypeStruct((tn, m), BF16),
            jax.ShapeDtypeStruct((m, tn), BF16),
        ],
        scratch_shapes=[
            pltpu.VMEM((tm, d), BF16),
            pltpu.VMEM((tm, tn), F32),
            pltpu.VMEM((tm, tn), F32),
            pltpu.VMEM((SUBLANES, tn), F32),
        ],
        compiler_params=_params("arbitrary", "arbitrary"),
        name="norm_in_proj_conv",
    )(x, g, w, conv_w, conv_gain)


def _attn_kernel(q_ref, k_ref, vt_ref, bias_ref, lq1_ref, lk1_ref, lq2_ref, lk2_ref, g_ref,
                 o_ref, s_ref, m_ref, alpha_ref, acc_ref, *, kt, lambda_init):
    qi = pl.program_id(2)
    n_blocks = 2 * kt // QUERY_BLOCK
    all_q = range(n_blocks)
    block = lambda b: slice(b * QUERY_BLOCK, (b + 1) * QUERY_BLOCK)

    q = (q_ref[0].astype(F32) * (QK_DIM ** -0.5 * LOG2E)).astype(BF16)
    lane = lax.broadcasted_iota(jnp.int32, q.shape, 1)
    zero = jnp.zeros_like(q)
    q_halves = (jnp.where(lane < QK_DIM, q, zero), jnp.where(lane >= QK_DIM, q, zero))

    m_ref[...] = jnp.full(m_ref.shape, MASK_VALUE, F32)
    acc_ref[...] = jnp.zeros(acc_ref.shape, F32)

    def qk(t, b, nk):
        k_tile = k_ref[0, pl.ds(pl.multiple_of(t * kt, kt), nk), :]
        return [lax.dot_general(k_tile, q_halves[c][block(b)], (((1,), (1,)), ((), ())),
                                preferred_element_type=F32) for c in range(2)]

    def put_scores(scores, slot, bias_tile, b, nk):
        for c, s in enumerate(scores):
            if bias_tile is not None:
                s = s + bias_ref[0, bias_tile, b, :nk, :]
            m_prev = m_ref[1 - slot, c, b]
            m_new = jnp.maximum(m_prev, jnp.max(s, axis=0, keepdims=True))
            alpha_ref[slot, c, b] = jnp.exp2(m_prev - m_new)
            m_ref[slot, c, b] = m_new
            for h in range(QUERY_BLOCK // LANES):
                s_ref[slot, c, b, h, :nk, :] = s[:, h * LANES:(h + 1) * LANES]
                if nk < kt:
                    s_ref[slot, c, b, h, nk:, :] = jnp.full((kt - nk, LANES), MASK_VALUE, F32)

    def accumulate(t, slot, b, nk):
        vt = vt_ref[:, pl.ds(pl.multiple_of(t * kt, kt), nk)]
        vt = jnp.concatenate([vt, jnp.ones((SUM_ROWS, nk), BF16)], axis=0)
        for c in range(2):
            s = jnp.concatenate([s_ref[slot, c, b, h, :nk, :] for h in range(QUERY_BLOCK // LANES)],
                                axis=1)
            p = jnp.exp2(s - m_ref[slot, c, b])
            acc_ref[c, b] = alpha_ref[slot, c, b] * acc_ref[c, b] + jnp.dot(
                vt, p.astype(BF16), preferred_element_type=F32)

    def visible_keys(kind, b):
        if kind not in (1, 2):
            return kt
        return min(max(QUERY_BLOCK * (b + 1) - (kind - 1) * kt, 0), kt)

    def stage(nxt, slot, nxt_kind, cur, cur_kind):
        for b in all_q:
            scores = qk(nxt, b, visible_keys(nxt_kind, b))
            if visible_keys(cur_kind, b):
                accumulate(cur, 1 - slot, b, visible_keys(cur_kind, b))
            put_scores(scores, slot, nxt_kind, b, visible_keys(nxt_kind, b))

    for b in all_q:
        if visible_keys(2, b):
            put_scores(qk(2 * qi + 1, b, visible_keys(2, b)), 0, 2, b, visible_keys(2, b))
    stage(2 * qi, 1, 1, 2 * qi + 1, 2)

    @pl.when(qi >= 1)
    def _():
        stage(2 * qi - 1, 0, 0, 2 * qi, 1)
        stage(2 * qi - 2, 1, None, 2 * qi - 1, 0)

    def far_tiles(first, count, in_slot1):
        for t in range(0, count, 2):
            stage(first + t, 0, None, in_slot1, None)
            stage(first + t + 1, 1, None, first + t, None)
            in_slot1 = first + t + 1
        return in_slot1

    n_pairs = jnp.maximum(qi - 1, 0)
    in_slot1 = lax.fori_loop(0, n_pairs // 4, lambda p, in1: far_tiles(8 * p, 8, in1),
                             jnp.maximum(2 * qi - 2, 0))
    for pairs in (2, 1):
        first = 2 * (n_pairs - n_pairs % (2 * pairs))
        pending = n_pairs % (2 * pairs) >= pairs

        @pl.when(pending)
        def _():
            far_tiles(first, 2 * pairs, in_slot1)

        in_slot1 = jnp.where(pending, first + 2 * pairs - 1, in_slot1)

    lam = (jnp.exp(jnp.sum(lq1_ref[...] * lk1_ref[...], axis=-1, keepdims=True))
           - jnp.exp(jnp.sum(lq2_ref[...] * lk2_ref[...], axis=-1, keepdims=True))
           + lambda_init)
    for b in all_q:
        accumulate(in_slot1, 1, b, visible_keys(None, b))
        out = [acc_ref[c, b, :HEAD_DIM, :] / acc_ref[c, b, HEAD_DIM:HEAD_DIM + 1, :] for c in range(2)]
        a = out[0] - lam * out[1]
        a = a * lax.rsqrt(jnp.mean(a * a, axis=0, keepdims=True) + SUBLN_EPS)
        a = a * g_ref[...] * (1.0 - lambda_init)
        o_ref[0, block(b), :] = a.T.astype(o_ref.dtype)


def _attention(qk, vt, bias_tab, lq1, lk1, lq2, lk2, gain, *, lambda_init):
    b, s, _ = qk.shape
    kt = ATTN_K_TILE
    qt = 2 * kt
    nb = qt // QUERY_BLOCK
    vec = lambda n: pl.BlockSpec((1, n), lambda bi, h, qi: (0, 0))
    return pl.pallas_call(
        functools.partial(_attn_kernel, kt=kt, lambda_init=lambda_init),
        grid=(b, N_HEADS, s // qt),
        in_specs=[
            pl.BlockSpec((1, qt, HEAD_DIM), lambda bi, h, qi: (bi, qi, h)),
            pl.BlockSpec((1, s, HEAD_DIM), lambda bi, h, qi: (bi, 0, N_HEADS + h)),
            pl.BlockSpec((HEAD_DIM, s), lambda bi, h, qi: (h, bi)),
            pl.BlockSpec((1, 3, nb, kt, QUERY_BLOCK), lambda bi, h, qi: (h, 0, 0, 0, 0)),
            vec(QK_DIM), vec(QK_DIM), vec(QK_DIM), vec(QK_DIM),
            pl.BlockSpec((HEAD_DIM, 1), lambda bi, h, qi: (0, 0)),
        ],
        out_specs=pl.BlockSpec((1, qt, HEAD_DIM), lambda bi, h, qi: (bi, qi, h)),
        out_shape=jax.ShapeDtypeStruct((b, s, N_HEADS * HEAD_DIM), BF16),
        scratch_shapes=[
            pltpu.VMEM((2, 2, nb, QUERY_BLOCK // LANES, kt, LANES), F32),
            pltpu.VMEM((2, 2, nb, 1, QUERY_BLOCK), F32),
            pltpu.VMEM((2, 2, nb, 1, QUERY_BLOCK), F32),
            pltpu.VMEM((2, nb, HEAD_DIM + SUM_ROWS, QUERY_BLOCK), F32),
        ],
        compiler_params=_params("arbitrary", "arbitrary", "arbitrary"),
        name="diff_attention",
    )(qk, qk, vt, bias_tab, lq1, lk1, lq2, lk2, gain)


def _out_proj_kernel(a_ref, c_ref, w_ref, x_ref, g_ref, o_ref):
    aw = a_ref.shape[1]
    for rows in _row_chunks(a_ref.shape[0]):
        mix = jnp.dot(a_ref[rows, :], w_ref[:aw, :], preferred_element_type=F32)
        mix = mix + jnp.dot(c_ref[rows, :], w_ref[aw:, :], preferred_element_type=F32)
        o_ref[rows, :] = x_ref[rows, :] + _rms(mix, NORM_EPS) * g_ref[...]


def _out_proj(a, c, w, x, g, *, tm):
    m, d = x.shape
    row = lambda n: pl.BlockSpec((tm, n), lambda i: (i, 0))
    const = lambda shape: pl.BlockSpec(shape, lambda i: (0, 0))
    return pl.pallas_call(
        _out_proj_kernel,
        grid=(m // tm,),
        in_specs=[row(a.shape[1]), row(c.shape[1]), const(w.shape), row(d), const((1, d))],
        out_specs=row(d),
        out_shape=jax.ShapeDtypeStruct((m, d), F32),
        compiler_params=_params("arbitrary"),
        name="out_proj",
    )(a, c, w, x, g)


def _ffn_kernel(x_ref, gpre_ref, wg_ref, wu_ref, wd_ref, gpost_ref, o_ref, hn_ref):
    j = pl.program_id(1)
    last = pl.num_programs(1) - 1
    chunks = _row_chunks(x_ref.shape[0])

    def partial_out(hn):
        gate = jnp.dot(hn, wg_ref[...], preferred_element_type=F32)
        up = jnp.dot(hn, wu_ref[...], preferred_element_type=F32)
        h = (gate * jax.nn.sigmoid(gate) * up).astype(BF16)
        return jnp.dot(h, wd_ref[...], preferred_element_type=F32)

    @pl.when(j == 0)
    def _():
        for rows in chunks:
            hn = (_rms(x_ref[rows, :], NORM_EPS) * gpre_ref[...]).astype(BF16)
            hn_ref[rows, :] = hn
            o_ref[rows, :] = partial_out(hn)

    @pl.when((j > 0) & (j < last))
    def _():
        o_ref[...] += partial_out(hn_ref[...])

    @pl.when(j == last)
    def _():
        for rows in chunks:
            f = o_ref[rows, :] + partial_out(hn_ref[rows, :])
            o_ref[rows, :] = x_ref[rows, :] + _rms(f, NORM_EPS) * gpost_ref[...]


def _ffn(x, gpre, wg, wu, wd, gpost, *, tm, tf):
    m, d = x.shape
    f = wg.shape[1]
    assert f // tf >= 2
    return pl.pallas_call(
        _ffn_kernel,
        grid=(m // tm, f // tf),
        in_specs=[
            pl.BlockSpec((tm, d), lambda i, j: (i, 0)),
            pl.BlockSpec((1, d), lambda i, j: (0, 0)),
            pl.BlockSpec((d, tf), lambda i, j: (0, j)),
            pl.BlockSpec((d, tf), lambda i, j: (0, j)),
            pl.BlockSpec((tf, d), lambda i, j: (j, 0)),
            pl.BlockSpec((1, d), lambda i, j: (0, 0)),
        ],
        out_specs=pl.BlockSpec((tm, d), lambda i, j: (i, 0)),
        out_shape=jax.ShapeDtypeStruct((m, d), F32),
        scratch_shapes=[pltpu.VMEM((tm, d), BF16)],
        compiler_params=_params("arbitrary", "arbitrary"),
        name="swiglu_ffn",
    )(x, gpre, wg, wu, wd, gpost)


def kernel(x, w_in, w_out, lambda_q1, lambda_k1, lambda_q2, lambda_k2, subln_gain, conv_w, conv_norm_gain, rel_bias, w_gate, w_up, w_down, norm_mix_pre, norm_mix_post, norm_ffn_pre, norm_ffn_post):
    b, s, d = x.shape
    depth = w_in.shape[0]
    assert s % (2 * ATTN_K_TILE) == 0 and ATTN_K_TILE >= MAX_DISTANCE

    w_in, w_out, w_gate, w_up, w_down = (w.astype(BF16) for w in (w_in, w_out, w_gate, w_up, w_down))
    bias_tab = _bias_tiles(rel_bias, ATTN_K_TILE)
    row = lambda v: v.reshape(1, -1)

    xf = x.reshape(b * s, d)
    for l in range(depth):
        lambda_init = 0.8 - 0.6 * math.exp(-0.3 * l)
        qk, vt, c = _in_proj(xf, row(norm_mix_pre[l]), w_in[l], conv_w[l], row(conv_norm_gain[l]),
                             tm=1024, seq_len=s)
        a = _attention(qk.reshape(b, s, -1), vt, bias_tab, row(lambda_q1[l]), row(lambda_k1[l]),
                       row(lambda_q2[l]), row(lambda_k2[l]), subln_gain[l].reshape(-1, 1),
                       lambda_init=lambda_init)
        xf = _out_proj(a.reshape(b * s, -1), c, w_out[l], xf, row(norm_mix_post[l]), tm=512)
        xf = _ffn(xf, row(norm_ffn_pre[l]), w_gate[l], w_up[l], w_down[l], row(norm_ffn_post[l]),
                  tm=1024, tf=512)
    return xf.reshape(b, s, d)
```

```python
import functools
import math

import jax
import jax.numpy as jnp
from jax import lax
from jax.experimental import pallas as pl
from jax.experimental.pallas import tpu as pltpu

N_HEADS = 8
HEAD_DIM = 128
QK_DIM = 64
CONV_K = 3
N_BUCKETS = 32
MAX_DISTANCE = 128
NORM_EPS = 1e-6
SUBLN_EPS = 1e-5

LANES = 128
SUBLANES = 8
VMEM_LIMIT = 56 * 1024 * 1024

ROW_CHUNK = 256
ATTN_K_TILE = 512
SUM_ROWS = 16
QUERY_BLOCK = 256
MASK_VALUE = -1e30
LOG2E = math.log2(math.e)

F32 = jnp.float32
BF16 = jnp.bfloat16


def _params(*sem):
    return pltpu.CompilerParams(dimension_semantics=sem, vmem_limit_bytes=VMEM_LIMIT)


def _rms(x, eps):
    return x * lax.rsqrt(jnp.mean(x * x, axis=-1, keepdims=True) + eps)


def _row_chunks(rows, chunk=ROW_CHUNK):
    assert rows % chunk == 0
    return [slice(r, r + chunk) for r in range(0, rows, chunk)]


def _bias_window(rb_ref, h, key0, query0, shape):
    dist = (query0 - key0 + lax.broadcasted_iota(jnp.int32, shape, 1)
            - lax.broadcasted_iota(jnp.int32, shape, 0))
    n = jnp.maximum(dist, 0)
    max_exact = N_BUCKETS // 2
    nf = jnp.maximum(n, max_exact).astype(F32)
    large = max_exact + (jnp.log(nf / max_exact) / math.log(MAX_DISTANCE / max_exact)
                         * (N_BUCKETS - max_exact)).astype(jnp.int32)
    large = jnp.minimum(large, N_BUCKETS - 1)
    bucket = jnp.where(n < max_exact, n, large)
    val = jnp.zeros(shape, F32)
    for b in range(N_BUCKETS):
        val = jnp.where(bucket == b, rb_ref[b, h], val)
    val = (val - rb_ref[N_BUCKETS - 1, h]) * LOG2E
    return jnp.where(dist >= 0, val, MASK_VALUE)


def _bias_kernel(rb_ref, o_ref, *, kt):
    h = pl.program_id(0)
    _, n_tiles, n_blocks, _, _ = o_ref.shape
    qt = n_blocks * QUERY_BLOCK
    assert MAX_DISTANCE <= LANES
    for t in range(n_tiles):
        for k0 in range(0, kt, LANES):
            key0 = (t - 1) * kt + k0
            lo, hi = max(key0, 0), min(key0 + 2 * LANES, qt)
            parts = []
            if lo > 0:
                parts.append(jnp.full((LANES, lo), MASK_VALUE, F32))
            if hi > lo:
                parts.append(_bias_window(rb_ref, h, key0, lo, (LANES, hi - lo)))
            if max(hi, lo) < qt:
                parts.append(jnp.zeros((LANES, qt - max(hi, lo)), F32))
            strip = jnp.concatenate(parts, axis=1) if len(parts) > 1 else parts[0]
            for b in range(n_blocks):
                o_ref[0, t, b, k0:k0 + LANES, :] = strip[:, b * QUERY_BLOCK:(b + 1) * QUERY_BLOCK]


def _bias_tiles(rel_bias, kt):
    nb = 2 * kt // QUERY_BLOCK
    shape = (N_HEADS, 3, nb, kt, QUERY_BLOCK)
    return pl.pallas_call(
        functools.partial(_bias_kernel, kt=kt),
        grid=(N_HEADS,),
        in_specs=[pl.BlockSpec(memory_space=pltpu.SMEM)],
        out_specs=pl.BlockSpec((1,) + shape[1:], lambda h: (h, 0, 0, 0, 0)),
        out_shape=jax.ShapeDtypeStruct(shape, F32),
        compiler_params=_params("arbitrary"),
        name="rel_bias_tiles",
    )(rel_bias)


Q_BLOCK, K_BLOCK, V_BLOCK, GATE_B_BLOCK, GATE_C_BLOCK, H_BLOCK = range(6)


def _in_proj_kernel(x_ref, g_ref, w_ref, cw_ref, cg_ref, qk_ref, vt_ref, c_ref,
                    hn_ref, gb_ref, gc_ref, tail_ref, *, tiles_per_seq):
    i, j = pl.program_id(0), pl.program_id(1)
    chunks = _row_chunks(x_ref.shape[0])
    project = lambda hn: jnp.dot(hn, w_ref[...], preferred_element_type=F32)

    @pl.when(j == Q_BLOCK)
    def _():
        for rows in chunks:
            hn = (_rms(x_ref[rows, :], NORM_EPS) * g_ref[...]).astype(BF16)
            hn_ref[rows, :] = hn
            qk_ref[rows, :] = project(hn).astype(qk_ref.dtype)

    @pl.when(j == K_BLOCK)
    def _():
        qk_ref[...] = project(hn_ref[...]).astype(qk_ref.dtype)

    @pl.when(j == V_BLOCK)
    def _():
        for rows in chunks:
            vt_ref[:, rows] = project(hn_ref[rows, :]).T.astype(vt_ref.dtype)

    @pl.when(j == GATE_B_BLOCK)
    def _():
        gb_ref[...] = project(hn_ref[...])

    @pl.when(j == GATE_C_BLOCK)
    def _():
        gc_ref[...] = project(hn_ref[...])

    @pl.when(j == H_BLOCK)
    def _():
        above = jnp.where(i % tiles_per_seq == 0, 0.0, tail_ref[...])
        cw = cw_ref[...]
        row = lax.broadcasted_iota(jnp.int32, above.shape, 0)
        for rows in chunks:
            u = gc_ref[rows, :] * project(hn_ref[rows, :])
            u1, u2 = pltpu.roll(u, 1, 0), pltpu.roll(u, 2, 0)
            head1 = jnp.where(row == 0, above[7:8], u1[:SUBLANES])
            head2 = jnp.where(row == 0, above[6:7], jnp.where(row == 1, above[7:8], u2[:SUBLANES]))
            u1 = jnp.concatenate([head1, u1[SUBLANES:]], axis=0)
            u2 = jnp.concatenate([head2, u2[SUBLANES:]], axis=0)
            c = gb_ref[rows, :] * (cw[0:1] * u2 + cw[1:2] * u1 + cw[2:3] * u)
            c_ref[rows, :] = (_rms(c, NORM_EPS) * cg_ref[...]).astype(c_ref.dtype)
            above = u[-SUBLANES:]
        tail_ref[...] = above


def _in_proj(x, g, w, conv_w, conv_gain, *, tm, seq_len):
    m, d = x.shape
    tn = w.shape[1] // 6
    assert conv_w.shape[1] == tn and seq_len % tm == 0
    const = lambda shape: pl.BlockSpec(shape, lambda i, j: (0, 0))
    return pl.pallas_call(
        functools.partial(_in_proj_kernel, tiles_per_seq=seq_len // tm),
        grid=(m // tm, 6),
        in_specs=[
            pl.BlockSpec((tm, d), lambda i, j: (i, 0)),
            const((1, d)),
            pl.BlockSpec((d, tn), lambda i, j: (0, j)),
            const((CONV_K, tn)), const((1, tn)),
        ],
        out_specs=[
            pl.BlockSpec((tm, tn), lambda i, j: (i, jnp.minimum(j, K_BLOCK))),
            pl.BlockSpec((tn, tm), lambda i, j: (0, i)),
            pl.BlockSpec((tm, tn), lambda i, j: (i, 0)),
        ],
        out_shape=[
            jax.ShapeDtypeStruct((m, 2 * tn), BF16),
            jax.ShapeDtypeStruct((tn, m), BF16),
            jax.ShapeDtypeStruct((m, tn), BF16),
        ],
        scratch_shapes=[
            pltpu.VMEM((tm, d), BF16),
            pltpu.VMEM((tm, tn), F32),
            pltpu.VMEM((tm, tn), F32),
            pltpu.VMEM((SUBLANES, tn), F32),
        ],
        compiler_params=_params("arbitrary", "arbitrary"),
        name="norm_in_proj_conv",
    )(x, g, w, conv_w, conv_gain)


def _attn_kernel(q_ref, k_ref, vt_ref, bias_ref, lq1_ref, lk1_ref, lq2_ref, lk2_ref, g_ref,
                 o_ref, s_ref, m_ref, alpha_ref, acc_ref, *, kt, lambda_init):
    qi = pl.program_id(2)
    n_blocks = 2 * kt // QUERY_BLOCK
    all_q = range(n_blocks)
    block = lambda b: slice(b * QUERY_BLOCK, (b + 1) * QUERY_BLOCK)

    q = (q_ref[0].astype(F32) * (QK_DIM ** -0.5 * LOG2E)).astype(BF16)
    lane = lax.broadcasted_iota(jnp.int32, q.shape, 1)
    zero = jnp.zeros_like(q)
    q_halves = (jnp.where(lane < QK_DIM, q, zero), jnp.where(lane >= QK_DIM, q, zero))

    m_ref[...] = jnp.full(m_ref.shape, MASK_VALUE, F32)
    acc_ref[...] = jnp.zeros(acc_ref.shape, F32)

    def qk(t, b, nk):
        k_tile = k_ref[0, pl.ds(pl.multiple_of(t * kt, kt), nk), :]
        return [lax.dot_general(k_tile, q_halves[c][block(b)], (((1,), (1,)), ((), ())),
                                preferred_element_type=F32) for c in range(2)]

    def put_scores(scores, slot, bias_tile, b, nk):
        for c, s in enumerate(scores):
            if bias_tile is not None:
                s = s + bias_ref[0, bias_tile, b, :nk, :]
            m_prev = m_ref[1 - slot, c, b]
            m_new = jnp.maximum(m_prev, jnp.max(s, axis=0, keepdims=True))
            alpha_ref[slot, c, b] = jnp.exp2(m_prev - m_new)
            m_ref[slot, c, b] = m_new
            for h in range(QUERY_BLOCK // LANES):
                s_ref[slot, c, b, h, :nk, :] = s[:, h * LANES:(h + 1) * LANES]
                if nk < kt:
                    s_ref[slot, c, b, h, nk:, :] = jnp.full((kt - nk, LANES), MASK_VALUE, F32)

    def accumulate(t, slot, b, nk):
        vt = vt_ref[:, pl.ds(pl.multiple_of(t * kt, kt), nk)]
        vt = jnp.concatenate([vt, jnp.ones((SUM_ROWS, nk), BF16)], axis=0)
        for c in range(2):
            s = jnp.concatenate([s_ref[slot, c, b, h, :nk, :] for h in range(QUERY_BLOCK // LANES)],
                                axis=1)
            p = jnp.exp2(s - m_ref[slot, c, b])
            acc_ref[c, b] = alpha_ref[slot, c, b] * acc_ref[c, b] + jnp.dot(
                vt, p.astype(BF16), preferred_element_type=F32)

    def visible_keys(kind, b):
        if kind not in (1, 2):
            return kt
        return min(max(QUERY_BLOCK * (b + 1) - (kind - 1) * kt, 0), kt)

    def stage(nxt, slot, nxt_kind, cur, cur_kind):
        for b in all_q:
            scores = qk(nxt, b, visible_keys(nxt_kind, b))
            if visible_keys(cur_kind, b):
                accumulate(cur, 1 - slot, b, visible_keys(cur_kind, b))
            put_scores(scores, slot, nxt_kind, b, visible_keys(nxt_kind, b))

    for b in all_q:
        if visible_keys(2, b):
            put_scores(qk(2 * qi + 1, b, visible_keys(2, b)), 0, 2, b, visible_keys(2, b))
    stage(2 * qi, 1, 1, 2 * qi + 1, 2)

    @pl.when(qi >= 1)
    def _():
        stage(2 * qi - 1, 0, 0, 2 * qi, 1)
        stage(2 * qi - 2, 1, None, 2 * qi - 1, 0)

    def far_tiles(first, count, in_slot1):
        for t in range(0, count, 2):
            stage(first + t, 0, None, in_slot1, None)
            stage(first + t + 1, 1, None, first + t, None)
            in_slot1 = first + t + 1
        return in_slot1

    n_pairs = jnp.maximum(qi - 1, 0)
    in_slot1 = lax.fori_loop(0, n_pairs // 4, lambda p, in1: far_tiles(8 * p, 8, in1),
                             jnp.maximum(2 * qi - 2, 0))
    for pairs in (2, 1):
        first = 2 * (n_pairs - n_pairs % (2 * pairs))
        pending = n_pairs % (2 * pairs) >= pairs

        @pl.when(pending)
        def _():
            far_tiles(first, 2 * pairs, in_slot1)

        in_slot1 = jnp.where(pending, first + 2 * pairs - 1, in_slot1)

    lam = (jnp.exp(jnp.sum(lq1_ref[...] * lk1_ref[...], axis=-1, keepdims=True))
           - jnp.exp(jnp.sum(lq2_ref[...] * lk2_ref[...], axis=-1, keepdims=True))
           + lambda_init)
    for b in all_q:
        accumulate(in_slot1, 1, b, visible_keys(None, b))
        out = [acc_ref[c, b, :HEAD_DIM, :] / acc_ref[c, b, HEAD_DIM:HEAD_DIM + 1, :] for c in range(2)]
        a = out[0] - lam * out[1]
        a = a * lax.rsqrt(jnp.mean(a * a, axis=0, keepdims=True) + SUBLN_EPS)
        a = a * g_ref[...] * (1.0 - lambda_init)
        o_ref[0, block(b), :] = a.T.astype(o_ref.dtype)


def _attention(qk, vt, bias_tab, lq1, lk1, lq2, lk2, gain, *, lambda_init):
    b, s, _ = qk.shape
    kt = ATTN_K_TILE
    qt = 2 * kt
    nb = qt // QUERY_BLOCK
    vec = lambda n: pl.BlockSpec((1, n), lambda bi, h, qi: (0, 0))
    return pl.pallas_call(
        functools.partial(_attn_kernel, kt=kt, lambda_init=lambda_init),
        grid=(b, N_HEADS, s // qt),
        in_specs=[
            pl.BlockSpec((1, qt, HEAD_DIM), lambda bi, h, qi: (bi, qi, h)),
            pl.BlockSpec((1, s, HEAD_DIM), lambda bi, h, qi: (bi, 0, N_HEADS + h)),
            pl.BlockSpec((HEAD_DIM, s), lambda bi, h, qi: (h, bi)),
            pl.BlockSpec((1, 3, nb, kt, QUERY_BLOCK), lambda bi, h, qi: (h, 0, 0, 0, 0)),
            vec(QK_DIM), vec(QK_DIM), vec(QK_DIM), vec(QK_DIM),
            pl.BlockSpec((HEAD_DIM, 1), lambda bi, h, qi: (0, 0)),
        ],
        out_specs=pl.BlockSpec((1, qt, HEAD_DIM), lambda bi, h, qi: (bi, qi, h)),
        out_shape=jax.ShapeDtypeStruct((b, s, N_HEADS * HEAD_DIM), BF16),
        scratch_shapes=[
            pltpu.VMEM((2, 2, nb, QUERY_BLOCK // LANES, kt, LANES), F32),
            pltpu.VMEM((2, 2, nb, 1, QUERY_BLOCK), F32),
            pltpu.VMEM((2, 2, nb, 1, QUERY_BLOCK), F32),
            pltpu.VMEM((2, nb, HEAD_DIM + SUM_ROWS, QUERY_BLOCK), F32),
        ],
        compiler_params=_params("arbitrary", "arbitrary", "arbitrary"),
        name="diff_attention",
    )(qk, qk, vt, bias_tab, lq1, lk1, lq2, lk2, gain)


def _out_proj_kernel(a_ref, c_ref, w_ref, x_ref, g_ref, o_ref):
    aw = a_ref.shape[1]
    for rows in _row_chunks(a_ref.shape[0], 2 * ROW_CHUNK):
        mix = jnp.dot(a_ref[rows, :], w_ref[:aw, :], preferred_element_type=F32)
        mix = mix + jnp.dot(c_ref[rows, :], w_ref[aw:, :], preferred_element_type=F32)
        o_ref[rows, :] = x_ref[rows, :] + _rms(mix, NORM_EPS) * g_ref[...]


def _out_proj(a, c, w, x, g, *, tm):
    m, d = x.shape
    row = lambda n: pl.BlockSpec((tm, n), lambda i: (i, 0))
    const = lambda shape: pl.BlockSpec(shape, lambda i: (0, 0))
    resident = pl.BlockSpec(w.shape, lambda i: (0, 0), pipeline_mode=pl.Buffered(1))
    return pl.pallas_call(
        _out_proj_kernel,
        grid=(m // tm,),
        in_specs=[row(a.shape[1]), row(c.shape[1]), resident, row(d), const((1, d))],
        out_specs=row(d),
        out_shape=jax.ShapeDtypeStruct((m, d), F32),
        compiler_params=_params("arbitrary"),
        name="out_proj",
    )(a, c, w, x, g)


def _ffn_kernel(x_ref, gpre_ref, wg_ref, wu_ref, wd_ref, gpost_ref, o_ref, hn_ref):
    j = pl.program_id(1)
    last = pl.num_programs(1) - 1
    chunks = _row_chunks(x_ref.shape[0], 2 * ROW_CHUNK)

    def partial_out(hn):
        gate = jnp.dot(hn, wg_ref[...], preferred_element_type=F32)
        up = jnp.dot(hn, wu_ref[...], preferred_element_type=F32)
        h = (gate * jax.nn.sigmoid(gate) * up).astype(BF16)
        return jnp.dot(h, wd_ref[...], preferred_element_type=F32)

    @pl.when(j == 0)
    def _():
        for rows in chunks:
            hn = (_rms(x_ref[rows, :], NORM_EPS) * gpre_ref[...]).astype(BF16)
            hn_ref[rows, :] = hn
            o_ref[rows, :] = partial_out(hn)

    @pl.when((j > 0) & (j < last))
    def _():
        o_ref[...] += partial_out(hn_ref[...])

    @pl.when(j == last)
    def _():
        for rows in chunks:
            f = o_ref[rows, :] + partial_out(hn_ref[rows, :])
            o_ref[rows, :] = x_ref[rows, :] + _rms(f, NORM_EPS) * gpost_ref[...]


def _ffn(x, gpre, wg, wu, wd, gpost, *, tm, tf):
    m, d = x.shape
    f = wg.shape[1]
    assert f // tf >= 2
    return pl.pallas_call(
        _ffn_kernel,
        grid=(m // tm, f // tf),
        in_specs=[
            pl.BlockSpec((tm, d), lambda i, j: (i, 0)),
            pl.BlockSpec((1, d), lambda i, j: (0, 0)),
            pl.BlockSpec((d, tf), lambda i, j: (0, j)),
            pl.BlockSpec((d, tf), lambda i, j: (0, j)),
            pl.BlockSpec((tf, d), lambda i, j: (j, 0)),
            pl.BlockSpec((1, d), lambda i, j: (0, 0)),
        ],
        out_specs=pl.BlockSpec((tm, d), lambda i, j: (i, 0)),
        out_shape=jax.ShapeDtypeStruct((m, d), F32),
        scratch_shapes=[pltpu.VMEM((tm, d), BF16)],
        compiler_params=_params("arbitrary", "arbitrary"),
        name="swiglu_ffn",
    )(x, gpre, wg, wu, wd, gpost)


def kernel(x, w_in, w_out, lambda_q1, lambda_k1, lambda_q2, lambda_k2, subln_gain, conv_w, conv_norm_gain, rel_bias, w_gate, w_up, w_down, norm_mix_pre, norm_mix_post, norm_ffn_pre, norm_ffn_post):
    b, s, d = x.shape
    depth = w_in.shape[0]
    assert s % (2 * ATTN_K_TILE) == 0 and ATTN_K_TILE >= MAX_DISTANCE

    w_in, w_out, w_gate, w_up, w_down = (w.astype(BF16) for w in (w_in, w_out, w_gate, w_up, w_down))
    bias_tab = _bias_tiles(rel_bias, ATTN_K_TILE)
    row = lambda v: v.reshape(1, -1)

    xf = x.reshape(b * s, d)
    for l in range(depth):
        lambda_init = 0.8 - 0.6 * math.exp(-0.3 * l)
        qk, vt, c = _in_proj(xf, row(norm_mix_pre[l]), w_in[l], conv_w[l], row(conv_norm_gain[l]),
                             tm=1024, seq_len=s)
        a = _attention(qk.reshape(b, s, -1), vt, bias_tab, row(lambda_q1[l]), row(lambda_k1[l]),
                       row(lambda_q2[l]), row(lambda_k2[l]), subln_gain[l].reshape(-1, 1),
                       lambda_init=lambda_init)
        xf = _out_proj(a.reshape(b * s, -1), c, w_out[l], xf, row(norm_mix_post[l]), tm=1024)
        xf = _ffn(xf, row(norm_ffn_pre[l]), w_gate[l], w_up[l], w_down[l], row(norm_ffn_post[l]),
                  tm=1024, tf=512)
    return xf.reshape(b, s, d)
```

```python
import functools
import math

import jax
import jax.numpy as jnp
from jax import lax
from jax.experimental import pallas as pl
from jax.experimental.pallas import tpu as pltpu

N_HEADS = 8
HEAD_DIM = 128
QK_DIM = 64
CONV_K = 3
N_BUCKETS = 32
MAX_DISTANCE = 128
NORM_EPS = 1e-6
SUBLN_EPS = 1e-5

LANES = 128
SUBLANES = 8
VMEM_LIMIT = 56 * 1024 * 1024

ROW_TILE = 1024
FFN_CHUNK = 512
ROW_CHUNK = 256
ATTN_K_TILE = 512
SUM_ROWS = 16
QUERY_BLOCK = 256
MASK_VALUE = -1e30
LOG2E = math.log2(math.e)

F32 = jnp.float32
BF16 = jnp.bfloat16


def _params(*sem):
    return pltpu.CompilerParams(dimension_semantics=sem, vmem_limit_bytes=VMEM_LIMIT)


def _rms(x, eps):
    return x * lax.rsqrt(jnp.mean(x * x, axis=-1, keepdims=True) + eps)


def _row_chunks(rows, chunk=ROW_CHUNK):
    assert rows % chunk == 0
    return [slice(r, r + chunk) for r in range(0, rows, chunk)]


def _bias_window(rb_ref, h, key0, query0, shape):
    dist = (query0 - key0 + lax.broadcasted_iota(jnp.int32, shape, 1)
            - lax.broadcasted_iota(jnp.int32, shape, 0))
    n = jnp.maximum(dist, 0)
    max_exact = N_BUCKETS // 2
    nf = jnp.maximum(n, max_exact).astype(F32)
    large = max_exact + (jnp.log(nf / max_exact) / math.log(MAX_DISTANCE / max_exact)
                         * (N_BUCKETS - max_exact)).astype(jnp.int32)
    large = jnp.minimum(large, N_BUCKETS - 1)
    bucket = jnp.where(n < max_exact, n, large)
    val = jnp.zeros(shape, F32)
    for b in range(N_BUCKETS):
        val = jnp.where(bucket == b, rb_ref[b, h], val)
    val = (val - rb_ref[N_BUCKETS - 1, h]) * LOG2E
    return jnp.where(dist >= 0, val, MASK_VALUE)


def _bias_kernel(rb_ref, o_ref, *, kt):
    h = pl.program_id(0)
    _, n_tiles, n_blocks, _, _ = o_ref.shape
    qt = n_blocks * QUERY_BLOCK
    assert MAX_DISTANCE <= LANES
    for t in range(n_tiles):
        for k0 in range(0, kt, LANES):
            key0 = (t - 1) * kt + k0
            lo, hi = max(key0, 0), min(key0 + 2 * LANES, qt)
            parts = []
            if lo > 0:
                parts.append(jnp.full((LANES, lo), MASK_VALUE, F32))
            if hi > lo:
                parts.append(_bias_window(rb_ref, h, key0, lo, (LANES, hi - lo)))
            if max(hi, lo) < qt:
                parts.append(jnp.zeros((LANES, qt - max(hi, lo)), F32))
            strip = jnp.concatenate(parts, axis=1) if len(parts) > 1 else parts[0]
            for b in range(n_blocks):
                o_ref[0, t, b, k0:k0 + LANES, :] = strip[:, b * QUERY_BLOCK:(b + 1) * QUERY_BLOCK]


def _bias_tiles(rel_bias, kt):
    nb = 2 * kt // QUERY_BLOCK
    shape = (N_HEADS, 3, nb, kt, QUERY_BLOCK)
    return pl.pallas_call(
        functools.partial(_bias_kernel, kt=kt),
        grid=(N_HEADS,),
        in_specs=[pl.BlockSpec(memory_space=pltpu.SMEM)],
        out_specs=pl.BlockSpec((1,) + shape[1:], lambda h: (h, 0, 0, 0, 0)),
        out_shape=jax.ShapeDtypeStruct(shape, F32),
        compiler_params=_params("arbitrary"),
        name="rel_bias_tiles",
    )(rel_bias)


Q_BLOCK, K_BLOCK, V_BLOCK, GATE_B_BLOCK, GATE_C_BLOCK, H_BLOCK = range(6)


def _in_proj_kernel(x_ref, g_ref, w_ref, cw_ref, cg_ref, qk_ref, vt_ref, c_ref,
                    hn_ref, gb_ref, gc_ref, tail_ref, *, tiles_per_seq):
    i, j = pl.program_id(0), pl.program_id(1)
    chunks = _row_chunks(x_ref.shape[0])
    project = lambda hn: jnp.dot(hn, w_ref[...], preferred_element_type=F32)

    @pl.when(j == Q_BLOCK)
    def _():
        for rows in chunks:
            hn = (_rms(x_ref[rows, :], NORM_EPS) * g_ref[...]).astype(BF16)
            hn_ref[rows, :] = hn
            qk_ref[rows, :] = project(hn).astype(qk_ref.dtype)

    @pl.when(j == K_BLOCK)
    def _():
        qk_ref[...] = project(hn_ref[...]).astype(qk_ref.dtype)

    @pl.when(j == V_BLOCK)
    def _():
        for rows in chunks:
            vt_ref[:, rows] = project(hn_ref[rows, :]).T.astype(vt_ref.dtype)

    @pl.when(j == GATE_B_BLOCK)
    def _():
        gb_ref[...] = project(hn_ref[...])

    @pl.when(j == GATE_C_BLOCK)
    def _():
        gc_ref[...] = project(hn_ref[...])

    @pl.when(j == H_BLOCK)
    def _():
        above = jnp.where(i % tiles_per_seq == 0, 0.0, tail_ref[...])
        cw = cw_ref[...]
        row = lax.broadcasted_iota(jnp.int32, above.shape, 0)
        for rows in chunks:
            u = gc_ref[rows, :] * project(hn_ref[rows, :])
            u1, u2 = pltpu.roll(u, 1, 0), pltpu.roll(u, 2, 0)
            head1 = jnp.where(row == 0, above[7:8], u1[:SUBLANES])
            head2 = jnp.where(row == 0, above[6:7], jnp.where(row == 1, above[7:8], u2[:SUBLANES]))
            u1 = jnp.concatenate([head1, u1[SUBLANES:]], axis=0)
            u2 = jnp.concatenate([head2, u2[SUBLANES:]], axis=0)
            c = gb_ref[rows, :] * (cw[0:1] * u2 + cw[1:2] * u1 + cw[2:3] * u)
            c_ref[rows, :] = (_rms(c, NORM_EPS) * cg_ref[...]).astype(c_ref.dtype)
            above = u[-SUBLANES:]
        tail_ref[...] = above


def _in_proj(x, g, w, conv_w, conv_gain, *, tm, seq_len):
    m, d = x.shape
    tn = w.shape[1] // 6
    assert conv_w.shape[1] == tn and seq_len % tm == 0
    const = lambda shape: pl.BlockSpec(shape, lambda i, j: (0, 0))
    return pl.pallas_call(
        functools.partial(_in_proj_kernel, tiles_per_seq=seq_len // tm),
        grid=(m // tm, 6),
        in_specs=[
            pl.BlockSpec((tm, d), lambda i, j: (i, 0)),
            const((1, d)),
            pl.BlockSpec((d, tn), lambda i, j: (0, j)),
            const((CONV_K, tn)), const((1, tn)),
        ],
        out_specs=[
            pl.BlockSpec((tm, tn), lambda i, j: (i, jnp.minimum(j, K_BLOCK))),
            pl.BlockSpec((tn, tm), lambda i, j: (0, i)),
            pl.BlockSpec((tm, tn), lambda i, j: (i, 0)),
        ],
        out_shape=[
            jax.ShapeDtypeStruct((m, 2 * tn), BF16),
            jax.ShapeDtypeStruct((tn, m), BF16),
            jax.ShapeDtypeStruct((m, tn), BF16),
        ],
        scratch_shapes=[
            pltpu.VMEM((tm, d), BF16),
            pltpu.VMEM((tm, tn), F32),
            pltpu.VMEM((tm, tn), F32),
            pltpu.VMEM((SUBLANES, tn), F32),
        ],
        compiler_params=_params("arbitrary", "arbitrary"),
        name="norm_in_proj_conv",
    )(x, g, w, conv_w, conv_gain)


def _attn_kernel(q_ref, k_ref, vt_ref, bias_ref, lq1_ref, lk1_ref, lq2_ref, lk2_ref, g_ref,
                 o_ref, s_ref, m_ref, alpha_ref, acc_ref, *, kt, lambda_init):
    qi = pl.program_id(2)
    n_blocks = 2 * kt // QUERY_BLOCK
    all_q = range(n_blocks)
    block = lambda b: slice(b * QUERY_BLOCK, (b + 1) * QUERY_BLOCK)

    q = (q_ref[0].astype(F32) * (QK_DIM ** -0.5 * LOG2E)).astype(BF16)
    lane = lax.broadcasted_iota(jnp.int32, q.shape, 1)
    zero = jnp.zeros_like(q)
    q_halves = (jnp.where(lane < QK_DIM, q, zero), jnp.where(lane >= QK_DIM, q, zero))

    m_ref[...] = jnp.full(m_ref.shape, MASK_VALUE, F32)
    acc_ref[...] = jnp.zeros(acc_ref.shape, F32)

    def qk(t, b, nk):
        k_tile = k_ref[0, pl.ds(pl.multiple_of(t * kt, kt), nk), :]
        return [lax.dot_general(k_tile, q_halves[c][block(b)], (((1,), (1,)), ((), ())),
                                preferred_element_type=F32) for c in range(2)]

    def has_bias(bias_tile, b):
        if bias_tile is None:
            return False
        first_key = (bias_tile - 1) * kt - QUERY_BLOCK * b
        return first_key + kt - 1 > -MAX_DISTANCE

    def put_scores(scores, slot, bias_tile, b, nk):
        for c, s in enumerate(scores):
            if has_bias(bias_tile, b):
                s = s + bias_ref[0, bias_tile, b, :nk, :]
            m_prev = m_ref[1 - slot, c, b]
            m_new = jnp.maximum(m_prev, jnp.max(s, axis=0, keepdims=True))
            alpha_ref[slot, c, b] = jnp.exp2(m_prev - m_new)
            m_ref[slot, c, b] = m_new
            for h in range(QUERY_BLOCK // LANES):
                s_ref[slot, c, b, h, :nk, :] = s[:, h * LANES:(h + 1) * LANES]
                if nk < kt:
                    s_ref[slot, c, b, h, nk:, :] = jnp.full((kt - nk, LANES), MASK_VALUE, F32)

    def accumulate(t, slot, b, nk):
        vt = vt_ref[:, pl.ds(pl.multiple_of(t * kt, kt), nk)]
        vt = jnp.concatenate([vt, jnp.ones((SUM_ROWS, nk), BF16)], axis=0)
        for c in range(2):
            s = jnp.concatenate([s_ref[slot, c, b, h, :nk, :] for h in range(QUERY_BLOCK // LANES)],
                                axis=1)
            p = jnp.exp2(s - m_ref[slot, c, b])
            acc_ref[c, b] = alpha_ref[slot, c, b] * acc_ref[c, b] + jnp.dot(
                vt, p.astype(BF16), preferred_element_type=F32)

    def visible_keys(kind, b):
        if kind not in (1, 2):
            return kt
        return min(max(QUERY_BLOCK * (b + 1) - (kind - 1) * kt, 0), kt)

    def stage(nxt, slot, nxt_kind, cur, cur_kind):
        for b in all_q:
            scores = qk(nxt, b, visible_keys(nxt_kind, b))
            if visible_keys(cur_kind, b):
                accumulate(cur, 1 - slot, b, visible_keys(cur_kind, b))
            put_scores(scores, slot, nxt_kind, b, visible_keys(nxt_kind, b))

    for b in all_q:
        if visible_keys(2, b):
            put_scores(qk(2 * qi + 1, b, visible_keys(2, b)), 0, 2, b, visible_keys(2, b))
    stage(2 * qi, 1, 1, 2 * qi + 1, 2)

    @pl.when(qi >= 1)
    def _():
        stage(2 * qi - 1, 0, 0, 2 * qi, 1)
        stage(2 * qi - 2, 1, None, 2 * qi - 1, 0)

    def far_tiles(first, count, in_slot1):
        for t in range(0, count, 2):
            stage(first + t, 0, None, in_slot1, None)
            stage(first + t + 1, 1, None, first + t, None)
            in_slot1 = first + t + 1
        return in_slot1

    n_pairs = jnp.maximum(qi - 1, 0)
    in_slot1 = lax.fori_loop(0, n_pairs // 4, lambda p, in1: far_tiles(8 * p, 8, in1),
                             jnp.maximum(2 * qi - 2, 0))
    for pairs in (2, 1):
        first = 2 * (n_pairs - n_pairs % (2 * pairs))
        pending = n_pairs % (2 * pairs) >= pairs

        @pl.when(pending)
        def _():
            far_tiles(first, 2 * pairs, in_slot1)

        in_slot1 = jnp.where(pending, first + 2 * pairs - 1, in_slot1)

    lam = (jnp.exp(jnp.sum(lq1_ref[...] * lk1_ref[...], axis=-1, keepdims=True))
           - jnp.exp(jnp.sum(lq2_ref[...] * lk2_ref[...], axis=-1, keepdims=True))
           + lambda_init)
    for b in all_q:
        accumulate(in_slot1, 1, b, visible_keys(None, b))
        out = [acc_ref[c, b, :HEAD_DIM, :] / acc_ref[c, b, HEAD_DIM:HEAD_DIM + 1, :] for c in range(2)]
        a = out[0] - lam * out[1]
        a = a * lax.rsqrt(jnp.mean(a * a, axis=0, keepdims=True) + SUBLN_EPS)
        a = a * g_ref[...] * (1.0 - lambda_init)
        o_ref[0, block(b), :] = a.T.astype(o_ref.dtype)


def _attention(qk, vt, bias_tab, lq1, lk1, lq2, lk2, gain, *, lambda_init):
    b, s, _ = qk.shape
    kt = ATTN_K_TILE
    qt = 2 * kt
    nb = qt // QUERY_BLOCK
    vec = lambda n: pl.BlockSpec((1, n), lambda bi, h, qi: (0, 0))
    return pl.pallas_call(
        functools.partial(_attn_kernel, kt=kt, lambda_init=lambda_init),
        grid=(b, N_HEADS, s // qt),
        in_specs=[
            pl.BlockSpec((1, qt, HEAD_DIM), lambda bi, h, qi: (bi, qi, h)),
            pl.BlockSpec((1, s, HEAD_DIM), lambda bi, h, qi: (bi, 0, N_HEADS + h)),
            pl.BlockSpec((HEAD_DIM, s), lambda bi, h, qi: (h, bi)),
            pl.BlockSpec((1, 3, nb, kt, QUERY_BLOCK), lambda bi, h, qi: (h, 0, 0, 0, 0)),
            vec(QK_DIM), vec(QK_DIM), vec(QK_DIM), vec(QK_DIM),
            pl.BlockSpec((HEAD_DIM, 1), lambda bi, h, qi: (0, 0)),
        ],
        out_specs=pl.BlockSpec((1, qt, HEAD_DIM), lambda bi, h, qi: (bi, qi, h)),
        out_shape=jax.ShapeDtypeStruct((b, s, N_HEADS * HEAD_DIM), BF16),
        scratch_shapes=[
            pltpu.VMEM((2, 2, nb, QUERY_BLOCK // LANES, kt, LANES), F32),
            pltpu.VMEM((2, 2, nb, 1, QUERY_BLOCK), F32),
            pltpu.VMEM((2, 2, nb, 1, QUERY_BLOCK), F32),
            pltpu.VMEM((2, nb, HEAD_DIM + SUM_ROWS, QUERY_BLOCK), F32),
        ],
        compiler_params=_params("arbitrary", "arbitrary", "arbitrary"),
        name="diff_attention",
    )(qk, qk, vt, bias_tab, lq1, lk1, lq2, lk2, gain)


def _out_proj_kernel(a_ref, c_ref, w_ref, x_ref, g_ref, o_ref):
    aw = a_ref.shape[1]
    for rows in _row_chunks(a_ref.shape[0], 2 * ROW_CHUNK):
        mix = jnp.dot(a_ref[rows, :], w_ref[:aw, :], preferred_element_type=F32)
        mix = mix + jnp.dot(c_ref[rows, :], w_ref[aw:, :], preferred_element_type=F32)
        o_ref[rows, :] = x_ref[rows, :] + _rms(mix, NORM_EPS) * g_ref[...]


def _out_proj(a, c, w, x, g, *, tm):
    m, d = x.shape
    row = lambda n: pl.BlockSpec((tm, n), lambda i: (i, 0))
    const = lambda shape: pl.BlockSpec(shape, lambda i: (0, 0))
    resident = pl.BlockSpec(w.shape, lambda i: (0, 0), pipeline_mode=pl.Buffered(1))
    return pl.pallas_call(
        _out_proj_kernel,
        grid=(m // tm,),
        in_specs=[row(a.shape[1]), row(c.shape[1]), resident, row(d), const((1, d))],
        out_specs=row(d),
        out_shape=jax.ShapeDtypeStruct((m, d), F32),
        compiler_params=_params("arbitrary"),
        name="out_proj",
    )(a, c, w, x, g)


def _ffn_kernel(x_ref, gpre_ref, wg_ref, wu_ref, wd_ref, gpost_ref, o_ref, hn_ref):
    j = pl.program_id(1)
    last = pl.num_programs(1) - 1
    chunks = _row_chunks(x_ref.shape[0], 2 * ROW_CHUNK)

    def partial_out(hn):
        gate = jnp.dot(hn, wg_ref[...], preferred_element_type=F32)
        up = jnp.dot(hn, wu_ref[...], preferred_element_type=F32)
        h = (gate * jax.nn.sigmoid(gate) * up).astype(BF16)
        return jnp.dot(h, wd_ref[...], preferred_element_type=F32)

    @pl.when(j == 0)
    def _():
        for rows in chunks:
            hn = (_rms(x_ref[rows, :], NORM_EPS) * gpre_ref[...]).astype(BF16)
            hn_ref[rows, :] = hn
            o_ref[rows, :] = partial_out(hn)

    @pl.when((j > 0) & (j < last))
    def _():
        o_ref[...] += partial_out(hn_ref[...])

    @pl.when(j == last)
    def _():
        for rows in chunks:
            f = o_ref[rows, :] + partial_out(hn_ref[rows, :])
            o_ref[rows, :] = x_ref[rows, :] + _rms(f, NORM_EPS) * gpost_ref[...]


def _ffn(x, gpre, wg, wu, wd, gpost, *, tm, tf):
    m, d = x.shape
    f = wg.shape[1]
    assert f // tf >= 2
    return pl.pallas_call(
        _ffn_kernel,
        grid=(m // tm, f // tf),
        in_specs=[
            pl.BlockSpec((tm, d), lambda i, j: (i, 0)),
            pl.BlockSpec((1, d), lambda i, j: (0, 0)),
            pl.BlockSpec((d, tf), lambda i, j: (0, j)),
            pl.BlockSpec((d, tf), lambda i, j: (0, j)),
            pl.BlockSpec((tf, d), lambda i, j: (j, 0)),
            pl.BlockSpec((1, d), lambda i, j: (0, 0)),
        ],
        out_specs=pl.BlockSpec((tm, d), lambda i, j: (i, 0)),
        out_shape=jax.ShapeDtypeStruct((m, d), F32),
        scratch_shapes=[pltpu.VMEM((tm, d), BF16)],
        compiler_params=_params("arbitrary", "arbitrary"),
        name="swiglu_ffn",
    )(x, gpre, wg, wu, wd, gpost)


def kernel(x, w_in, w_out, lambda_q1, lambda_k1, lambda_q2, lambda_k2, subln_gain, conv_w, conv_norm_gain, rel_bias, w_gate, w_up, w_down, norm_mix_pre, norm_mix_post, norm_ffn_pre, norm_ffn_post):
    b, s, d = x.shape
    depth = w_in.shape[0]
    assert s % (2 * ATTN_K_TILE) == 0 and ATTN_K_TILE >= MAX_DISTANCE
    assert s % ROW_TILE == 0 and w_gate.shape[2] % FFN_CHUNK == 0
    assert w_in.shape[2] == 6 * N_HEADS * HEAD_DIM and 2 * QK_DIM == HEAD_DIM

    w_in, w_out, w_gate, w_up, w_down = (w.astype(BF16) for w in (w_in, w_out, w_gate, w_up, w_down))
    bias_tab = _bias_tiles(rel_bias, ATTN_K_TILE)
    row = lambda v: v.reshape(1, -1)

    xf = x.reshape(b * s, d)
    for l in range(depth):
        lambda_init = 0.8 - 0.6 * math.exp(-0.3 * l)
        qk, vt, c = _in_proj(xf, row(norm_mix_pre[l]), w_in[l], conv_w[l], row(conv_norm_gain[l]),
                             tm=ROW_TILE, seq_len=s)
        a = _attention(qk.reshape(b, s, -1), vt, bias_tab, row(lambda_q1[l]), row(lambda_k1[l]),
                       row(lambda_q2[l]), row(lambda_k2[l]), subln_gain[l].reshape(-1, 1),
                       lambda_init=lambda_init)
        xf = _out_proj(a.reshape(b * s, -1), c, w_out[l], xf, row(norm_mix_post[l]), tm=ROW_TILE)
        xf = _ffn(xf, row(norm_ffn_pre[l]), w_gate[l], w_up[l], w_down[l], row(norm_ffn_post[l]),
                  tm=ROW_TILE, tf=FFN_CHUNK)
    return xf.reshape(b, s, d)
```

```python
import functools
import math

import jax
import jax.numpy as jnp
from jax import lax
from jax.experimental import pallas as pl
from jax.experimental.pallas import tpu as pltpu

N_HEADS = 8
HEAD_DIM = 128
QK_DIM = 64
CONV_K = 3
N_BUCKETS = 32
MAX_DISTANCE = 128
NORM_EPS = 1e-6
SUBLN_EPS = 1e-5

LANES = 128
SUBLANES = 8
VMEM_LIMIT = 56 * 1024 * 1024

ROW_TILE = 1024
FFN_CHUNK = 512
ROW_CHUNK = 256
ATTN_K_TILE = 512
SUM_ROWS = 16
QUERY_BLOCK = 256
MASK_VALUE = -1e30
LOG2E = math.log2(math.e)

F32 = jnp.float32
BF16 = jnp.bfloat16


def _params(*sem):
    return pltpu.CompilerParams(dimension_semantics=sem, vmem_limit_bytes=VMEM_LIMIT)


def _rms(x, eps):
    return x * lax.rsqrt(jnp.mean(x * x, axis=-1, keepdims=True) + eps)


def _row_chunks(rows, chunk=ROW_CHUNK):
    assert rows % chunk == 0
    return [slice(r, r + chunk) for r in range(0, rows, chunk)]


def _bias_window(rb_ref, h, key0, query0, shape):
    dist = (query0 - key0 + lax.broadcasted_iota(jnp.int32, shape, 1)
            - lax.broadcasted_iota(jnp.int32, shape, 0))
    n = jnp.maximum(dist, 0)
    max_exact = N_BUCKETS // 2
    nf = jnp.maximum(n, max_exact).astype(F32)
    large = max_exact + (jnp.log(nf / max_exact) / math.log(MAX_DISTANCE / max_exact)
                         * (N_BUCKETS - max_exact)).astype(jnp.int32)
    large = jnp.minimum(large, N_BUCKETS - 1)
    bucket = jnp.where(n < max_exact, n, large)
    val = jnp.zeros(shape, F32)
    for b in range(N_BUCKETS):
        val = jnp.where(bucket == b, rb_ref[b, h], val)
    val = (val - rb_ref[N_BUCKETS - 1, h]) * LOG2E
    return jnp.where(dist >= 0, val, MASK_VALUE)


def _bias_kernel(rb_ref, o_ref, *, kt):
    h = pl.program_id(0)
    _, n_tiles, n_blocks, _, _ = o_ref.shape
    qt = n_blocks * QUERY_BLOCK
    assert MAX_DISTANCE <= LANES
    for t in range(n_tiles):
        for k0 in range(0, kt, LANES):
            key0 = (t - 1) * kt + k0
            lo, hi = max(key0, 0), min(key0 + 2 * LANES, qt)
            parts = []
            if lo > 0:
                parts.append(jnp.full((LANES, lo), MASK_VALUE, F32))
            if hi > lo:
                parts.append(_bias_window(rb_ref, h, key0, lo, (LANES, hi - lo)))
            if max(hi, lo) < qt:
                parts.append(jnp.zeros((LANES, qt - max(hi, lo)), F32))
            strip = jnp.concatenate(parts, axis=1) if len(parts) > 1 else parts[0]
            for b in range(n_blocks):
                o_ref[0, t, b, k0:k0 + LANES, :] = strip[:, b * QUERY_BLOCK:(b + 1) * QUERY_BLOCK]


def _bias_tiles(rel_bias, kt):
    nb = 2 * kt // QUERY_BLOCK
    shape = (N_HEADS, 3, nb, kt, QUERY_BLOCK)
    return pl.pallas_call(
        functools.partial(_bias_kernel, kt=kt),
        grid=(N_HEADS,),
        in_specs=[pl.BlockSpec(memory_space=pltpu.SMEM)],
        out_specs=pl.BlockSpec((1,) + shape[1:], lambda h: (h, 0, 0, 0, 0)),
        out_shape=jax.ShapeDtypeStruct(shape, F32),
        compiler_params=_params("arbitrary"),
        name="rel_bias_tiles",
    )(rel_bias)


Q_BLOCK, K_BLOCK, V_BLOCK, GATE_B_BLOCK, GATE_C_BLOCK, H_BLOCK = range(6)


def _in_proj_kernel(x_ref, g_ref, w_ref, cw_ref, cg_ref, qk_ref, vt_ref, c_ref,
                    hn_ref, gb_ref, gc_ref, tail_ref, *, tiles_per_seq):
    i, j = pl.program_id(0), pl.program_id(1)
    chunks = _row_chunks(x_ref.shape[0])
    project = lambda hn: jnp.dot(hn, w_ref[...], preferred_element_type=F32)

    @pl.when(j == Q_BLOCK)
    def _():
        for rows in chunks:
            hn = (_rms(x_ref[rows, :], NORM_EPS) * g_ref[...]).astype(BF16)
            hn_ref[rows, :] = hn
            qk_ref[rows, :] = project(hn).astype(qk_ref.dtype)

    @pl.when(j == K_BLOCK)
    def _():
        qk_ref[...] = project(hn_ref[...]).astype(qk_ref.dtype)

    @pl.when(j == V_BLOCK)
    def _():
        for rows in chunks:
            vt_ref[:, rows] = project(hn_ref[rows, :]).T.astype(vt_ref.dtype)

    @pl.when(j == GATE_B_BLOCK)
    def _():
        gb_ref[...] = project(hn_ref[...])

    @pl.when(j == GATE_C_BLOCK)
    def _():
        gc_ref[...] = project(hn_ref[...])

    @pl.when(j == H_BLOCK)
    def _():
        above = jnp.where(i % tiles_per_seq == 0, 0.0, tail_ref[...])
        cw = cw_ref[...]
        row = lax.broadcasted_iota(jnp.int32, above.shape, 0)
        for rows in chunks:
            u = gc_ref[rows, :] * project(hn_ref[rows, :])
            u1, u2 = pltpu.roll(u, 1, 0), pltpu.roll(u, 2, 0)
            head1 = jnp.where(row == 0, above[7:8], u1[:SUBLANES])
            head2 = jnp.where(row == 0, above[6:7], jnp.where(row == 1, above[7:8], u2[:SUBLANES]))
            u1 = jnp.concatenate([head1, u1[SUBLANES:]], axis=0)
            u2 = jnp.concatenate([head2, u2[SUBLANES:]], axis=0)
            c = gb_ref[rows, :] * (cw[0:1] * u2 + cw[1:2] * u1 + cw[2:3] * u)
            c_ref[rows, :] = (_rms(c, NORM_EPS) * cg_ref[...]).astype(c_ref.dtype)
            above = u[-SUBLANES:]
        tail_ref[...] = above


def _in_proj(x, g, w, conv_w, conv_gain, *, tm, seq_len):
    m, d = x.shape
    tn = w.shape[1] // 6
    assert conv_w.shape[1] == tn and seq_len % tm == 0
    const = lambda shape: pl.BlockSpec(shape, lambda i, j: (0, 0))
    return pl.pallas_call(
        functools.partial(_in_proj_kernel, tiles_per_seq=seq_len // tm),
        grid=(m // tm, 6),
        in_specs=[
            pl.BlockSpec((tm, d), lambda i, j: (i, 0)),
            const((1, d)),
            pl.BlockSpec((d, tn), lambda i, j: (0, j)),
            const((CONV_K, tn)), const((1, tn)),
        ],
        out_specs=[
            pl.BlockSpec((tm, tn), lambda i, j: (i, jnp.minimum(j, K_BLOCK))),
            pl.BlockSpec((tn, tm), lambda i, j: (0, i)),
            pl.BlockSpec((tm, tn), lambda i, j: (i, 0)),
        ],
        out_shape=[
            jax.ShapeDtypeStruct((m, 2 * tn), BF16),
            jax.ShapeDtypeStruct((tn, m), BF16),
            jax.ShapeDtypeStruct((m, tn), BF16),
        ],
        scratch_shapes=[
            pltpu.VMEM((tm, d), BF16),
            pltpu.VMEM((tm, tn), F32),
            pltpu.VMEM((tm, tn), F32),
            pltpu.VMEM((SUBLANES, tn), F32),
        ],
        compiler_params=_params("arbitrary", "arbitrary"),
        name="norm_in_proj_conv",
    )(x, g, w, conv_w, conv_gain)


def _attn_kernel(q_ref, k_ref, vt_ref, bias_ref, lq1_ref, lk1_ref, lq2_ref, lk2_ref, g_ref,
                 o_ref, s_ref, m_ref, alpha_ref, acc_ref, *, kt, lambda_init):
    qi = pl.program_id(2)
    n_blocks = 2 * kt // QUERY_BLOCK
    all_q = range(n_blocks)
    block = lambda b: slice(b * QUERY_BLOCK, (b + 1) * QUERY_BLOCK)

    q = (q_ref[0].astype(F32) * (QK_DIM ** -0.5 * LOG2E)).astype(BF16)
    lane = lax.broadcasted_iota(jnp.int32, q.shape, 1)
    zero = jnp.zeros_like(q)
    q_halves = (jnp.where(lane < QK_DIM, q, zero), jnp.where(lane >= QK_DIM, q, zero))

    m_ref[...] = jnp.full(m_ref.shape, MASK_VALUE, F32)
    acc_ref[...] = jnp.zeros(acc_ref.shape, F32)

    def qk(t, b, nk):
        k_tile = k_ref[0, pl.ds(pl.multiple_of(t * kt, kt), nk), :]
        return [lax.dot_general(k_tile, q_halves[c][block(b)], (((1,), (1,)), ((), ())),
                                preferred_element_type=F32) for c in range(2)]

    def has_bias(bias_tile, b):
        if bias_tile is None:
            return False
        first_key = (bias_tile - 1) * kt - QUERY_BLOCK * b
        return first_key + kt - 1 > -MAX_DISTANCE

    def put_scores(scores, slot, bias_tile, b, nk):
        for c, s in enumerate(scores):
            if has_bias(bias_tile, b):
                s = s + bias_ref[0, bias_tile, b, :nk, :]
            m_prev = m_ref[1 - slot, c, b]
            m_new = jnp.maximum(m_prev, jnp.max(s, axis=0, keepdims=True))
            alpha_ref[slot, c, b] = jnp.exp2(m_prev - m_new)
            m_ref[slot, c, b] = m_new
            for h in range(QUERY_BLOCK // LANES):
                s_ref[slot, c, b, h, :nk, :] = s[:, h * LANES:(h + 1) * LANES]
                if nk < kt:
                    s_ref[slot, c, b, h, nk:, :] = jnp.full((kt - nk, LANES), MASK_VALUE, F32)

    def values_t(t):
        vt = vt_ref[:, pl.ds(pl.multiple_of(t * kt, kt), kt)]
        return jnp.concatenate([vt, jnp.ones((SUM_ROWS, kt), BF16)], axis=0)

    def accumulate(vt, slot, b, nk):
        vt = vt[:, :nk]
        for c in range(2):
            s = jnp.concatenate([s_ref[slot, c, b, h, :nk, :] for h in range(QUERY_BLOCK // LANES)],
                                axis=1)
            p = jnp.exp2(s - m_ref[slot, c, b])
            acc_ref[c, b] = alpha_ref[slot, c, b] * acc_ref[c, b] + jnp.dot(
                vt, p.astype(BF16), preferred_element_type=F32)

    def visible_keys(kind, b):
        if kind not in (1, 2):
            return kt
        return min(max(QUERY_BLOCK * (b + 1) - (kind - 1) * kt, 0), kt)

    def stage(nxt, slot, nxt_kind, cur, cur_kind):
        vt = values_t(cur)
        for b in all_q:
            scores = qk(nxt, b, visible_keys(nxt_kind, b))
            if visible_keys(cur_kind, b):
                accumulate(vt, 1 - slot, b, visible_keys(cur_kind, b))
            put_scores(scores, slot, nxt_kind, b, visible_keys(nxt_kind, b))

    for b in all_q:
        if visible_keys(2, b):
            put_scores(qk(2 * qi + 1, b, visible_keys(2, b)), 0, 2, b, visible_keys(2, b))
    stage(2 * qi, 1, 1, 2 * qi + 1, 2)

    @pl.when(qi >= 1)
    def _():
        stage(2 * qi - 1, 0, 0, 2 * qi, 1)
        stage(2 * qi - 2, 1, None, 2 * qi - 1, 0)

    def far_tiles(first, count, in_slot1):
        for t in range(0, count, 2):
            stage(first + t, 0, None, in_slot1, None)
            stage(first + t + 1, 1, None, first + t, None)
            in_slot1 = first + t + 1
        return in_slot1

    n_pairs = jnp.maximum(qi - 1, 0)
    in_slot1 = lax.fori_loop(0, n_pairs // 4, lambda p, in1: far_tiles(8 * p, 8, in1),
                             jnp.maximum(2 * qi - 2, 0))
    for pairs in (2, 1):
        first = 2 * (n_pairs - n_pairs % (2 * pairs))
        pending = n_pairs % (2 * pairs) >= pairs

        @pl.when(pending)
        def _():
            far_tiles(first, 2 * pairs, in_slot1)

        in_slot1 = jnp.where(pending, first + 2 * pairs - 1, in_slot1)

    lam = (jnp.exp(jnp.sum(lq1_ref[...] * lk1_ref[...], axis=-1, keepdims=True))
           - jnp.exp(jnp.sum(lq2_ref[...] * lk2_ref[...], axis=-1, keepdims=True))
           + lambda_init)
    vt = values_t(in_slot1)
    for b in all_q:
        accumulate(vt, 1, b, visible_keys(None, b))
        out = [acc_ref[c, b, :HEAD_DIM, :] / acc_ref[c, b, HEAD_DIM:HEAD_DIM + 1, :] for c in range(2)]
        a = out[0] - lam * out[1]
        a = a * lax.rsqrt(jnp.mean(a * a, axis=0, keepdims=True) + SUBLN_EPS)
        a = a * g_ref[...] * (1.0 - lambda_init)
        o_ref[0, block(b), :] = a.T.astype(o_ref.dtype)


def _attention(qk, vt, bias_tab, lq1, lk1, lq2, lk2, gain, *, lambda_init):
    b, s, _ = qk.shape
    kt = ATTN_K_TILE
    qt = 2 * kt
    nb = qt // QUERY_BLOCK
    vec = lambda n: pl.BlockSpec((1, n), lambda bi, h, qi: (0, 0))
    return pl.pallas_call(
        functools.partial(_attn_kernel, kt=kt, lambda_init=lambda_init),
        grid=(b, N_HEADS, s // qt),
        in_specs=[
            pl.BlockSpec((1, qt, HEAD_DIM), lambda bi, h, qi: (bi, qi, h)),
            pl.BlockSpec((1, s, HEAD_DIM), lambda bi, h, qi: (bi, 0, N_HEADS + h)),
            pl.BlockSpec((HEAD_DIM, s), lambda bi, h, qi: (h, bi)),
            pl.BlockSpec((1, 3, nb, kt, QUERY_BLOCK), lambda bi, h, qi: (h, 0, 0, 0, 0)),
            vec(QK_DIM), vec(QK_DIM), vec(QK_DIM), vec(QK_DIM),
            pl.BlockSpec((HEAD_DIM, 1), lambda bi, h, qi: (0, 0)),
        ],
        out_specs=pl.BlockSpec((1, qt, HEAD_DIM), lambda bi, h, qi: (bi, qi, h)),
        out_shape=jax.ShapeDtypeStruct((b, s, N_HEADS * HEAD_DIM), BF16),
        scratch_shapes=[
            pltpu.VMEM((2, 2, nb, QUERY_BLOCK // LANES, kt, LANES), F32),
            pltpu.VMEM((2, 2, nb, 1, QUERY_BLOCK), F32),
            pltpu.VMEM((2, 2, nb, 1, QUERY_BLOCK), F32),
            pltpu.VMEM((2, nb, HEAD_DIM + SUM_ROWS, QUERY_BLOCK), F32),
        ],
        compiler_params=_params("arbitrary", "arbitrary", "arbitrary"),
        name="diff_attention",
    )(qk, qk, vt, bias_tab, lq1, lk1, lq2, lk2, gain)


def _out_proj_kernel(a_ref, c_ref, w_ref, x_ref, g_ref, o_ref):
    aw = a_ref.shape[1]
    for rows in _row_chunks(a_ref.shape[0], 2 * ROW_CHUNK):
        mix = jnp.dot(a_ref[rows, :], w_ref[:aw, :], preferred_element_type=F32)
        mix = mix + jnp.dot(c_ref[rows, :], w_ref[aw:, :], preferred_element_type=F32)
        o_ref[rows, :] = x_ref[rows, :] + _rms(mix, NORM_EPS) * g_ref[...]


def _out_proj(a, c, w, x, g, *, tm):
    m, d = x.shape
    row = lambda n: pl.BlockSpec((tm, n), lambda i: (i, 0))
    const = lambda shape: pl.BlockSpec(shape, lambda i: (0, 0))
    resident = pl.BlockSpec(w.shape, lambda i: (0, 0), pipeline_mode=pl.Buffered(1))
    return pl.pallas_call(
        _out_proj_kernel,
        grid=(m // tm,),
        in_specs=[row(a.shape[1]), row(c.shape[1]), resident, row(d), const((1, d))],
        out_specs=row(d),
        out_shape=jax.ShapeDtypeStruct((m, d), F32),
        compiler_params=_params("arbitrary"),
        name="out_proj",
    )(a, c, w, x, g)


def _ffn_kernel(x_ref, gpre_ref, wg_ref, wu_ref, wd_ref, gpost_ref, o_ref, hn_ref):
    j = pl.program_id(1)
    last = pl.num_programs(1) - 1
    chunks = _row_chunks(x_ref.shape[0], 2 * ROW_CHUNK)

    def partial_out(hn):
        gate = jnp.dot(hn, wg_ref[...], preferred_element_type=F32)
        up = jnp.dot(hn, wu_ref[...], preferred_element_type=F32)
        h = (gate * jax.nn.sigmoid(gate) * up).astype(BF16)
        return jnp.dot(h, wd_ref[...], preferred_element_type=F32)

    @pl.when(j == 0)
    def _():
        for rows in chunks:
            hn = (_rms(x_ref[rows, :], NORM_EPS) * gpre_ref[...]).astype(BF16)
            hn_ref[rows, :] = hn
            o_ref[rows, :] = partial_out(hn)

    @pl.when((j > 0) & (j < last))
    def _():
        o_ref[...] += partial_out(hn_ref[...])

    @pl.when(j == last)
    def _():
        for rows in chunks:
            f = o_ref[rows, :] + partial_out(hn_ref[rows, :])
            o_ref[rows, :] = x_ref[rows, :] + _rms(f, NORM_EPS) * gpost_ref[...]


def _ffn(x, gpre, wg, wu, wd, gpost, *, tm, tf):
    m, d = x.shape
    f = wg.shape[1]
    assert f // tf >= 2
    return pl.pallas_call(
        _ffn_kernel,
        grid=(m // tm, f // tf),
        in_specs=[
            pl.BlockSpec((tm, d), lambda i, j: (i, 0)),
            pl.BlockSpec((1, d), lambda i, j: (0, 0)),
            pl.BlockSpec((d, tf), lambda i, j: (0, j)),
            pl.BlockSpec((d, tf), lambda i, j: (0, j)),
            pl.BlockSpec((tf, d), lambda i, j: (j, 0)),
            pl.BlockSpec((1, d), lambda i, j: (0, 0)),
        ],
        out_specs=pl.BlockSpec((tm, d), lambda i, j: (i, 0)),
        out_shape=jax.ShapeDtypeStruct((m, d), F32),
        scratch_shapes=[pltpu.VMEM((tm, d), BF16)],
        compiler_params=_params("arbitrary", "arbitrary"),
        name="swiglu_ffn",
    )(x, gpre, wg, wu, wd, gpost)


def kernel(x, w_in, w_out, lambda_q1, lambda_k1, lambda_q2, lambda_k2, subln_gain, conv_w, conv_norm_gain, rel_bias, w_gate, w_up, w_down, norm_mix_pre, norm_mix_post, norm_ffn_pre, norm_ffn_post):
    b, s, d = x.shape
    depth = w_in.shape[0]
    assert s % (2 * ATTN_K_TILE) == 0 and ATTN_K_TILE >= MAX_DISTANCE
    assert s % ROW_TILE == 0 and w_gate.shape[2] % FFN_CHUNK == 0
    assert w_in.shape[2] == 6 * N_HEADS * HEAD_DIM and 2 * QK_DIM == HEAD_DIM

    w_in, w_out, w_gate, w_up, w_down = (w.astype(BF16) for w in (w_in, w_out, w_gate, w_up, w_down))
    bias_tab = _bias_tiles(rel_bias, ATTN_K_TILE)
    row = lambda v: v.reshape(1, -1)

    xf = x.reshape(b * s, d)
    for l in range(depth):
        lambda_init = 0.8 - 0.6 * math.exp(-0.3 * l)
        qk, vt, c = _in_proj(xf, row(norm_mix_pre[l]), w_in[l], conv_w[l], row(conv_norm_gain[l]),
                             tm=ROW_TILE, seq_len=s)
        a = _attention(qk.reshape(b, s, -1), vt, bias_tab, row(lambda_q1[l]), row(lambda_k1[l]),
                       row(lambda_q2[l]), row(lambda_k2[l]), subln_gain[l].reshape(-1, 1),
                       lambda_init=lambda_init)
        xf = _out_proj(a.reshape(b * s, -1), c, w_out[l], xf, row(norm_mix_post[l]), tm=ROW_TILE)
        xf = _ffn(xf, row(norm_ffn_pre[l]), w_gate[l], w_up[l], w_down[l], row(norm_ffn_post[l]),
                  tm=ROW_TILE, tf=FFN_CHUNK)
    return xf.reshape(b, s, d)
```

```python
import functools
import math

import jax
import jax.numpy as jnp
from jax import lax
from jax.experimental import pallas as pl
from jax.experimental.pallas import tpu as pltpu

N_HEADS = 8
HEAD_DIM = 128
QK_DIM = 64
CONV_K = 3
N_BUCKETS = 32
MAX_DISTANCE = 128
NORM_EPS = 1e-6
SUBLN_EPS = 1e-5

LANES = 128
SUBLANES = 8
VMEM_LIMIT = 56 * 1024 * 1024

ROW_TILE = 1024
FFN_CHUNK = 512
ROW_CHUNK = 256
ATTN_K_TILE = 512
SUM_ROWS = 16
QUERY_BLOCK = 256
MASK_VALUE = -1e30
LOG2E = math.log2(math.e)

F32 = jnp.float32
BF16 = jnp.bfloat16


def _params(*sem):
    return pltpu.CompilerParams(dimension_semantics=sem, vmem_limit_bytes=VMEM_LIMIT)


def _rms(x, eps):
    return x * lax.rsqrt(jnp.mean(x * x, axis=-1, keepdims=True) + eps)


def _row_chunks(rows, chunk=ROW_CHUNK):
    assert rows % chunk == 0
    return [slice(r, r + chunk) for r in range(0, rows, chunk)]


def _bias_window(rb_ref, h, key0, query0, shape):
    dist = (query0 - key0 + lax.broadcasted_iota(jnp.int32, shape, 1)
            - lax.broadcasted_iota(jnp.int32, shape, 0))
    n = jnp.maximum(dist, 0)
    max_exact = N_BUCKETS // 2
    nf = jnp.maximum(n, max_exact).astype(F32)
    large = max_exact + (jnp.log(nf / max_exact) / math.log(MAX_DISTANCE / max_exact)
                         * (N_BUCKETS - max_exact)).astype(jnp.int32)
    large = jnp.minimum(large, N_BUCKETS - 1)
    bucket = jnp.where(n < max_exact, n, large)
    val = jnp.zeros(shape, F32)
    for b in range(N_BUCKETS):
        val = jnp.where(bucket == b, rb_ref[b, h], val)
    val = (val - rb_ref[N_BUCKETS - 1, h]) * LOG2E
    return jnp.where(dist >= 0, val, MASK_VALUE)


def _bias_kernel(rb_ref, o_ref, *, kt):
    h = pl.program_id(0)
    _, n_tiles, n_blocks, _, _ = o_ref.shape
    qt = n_blocks * QUERY_BLOCK
    assert MAX_DISTANCE <= LANES
    for t in range(n_tiles):
        for k0 in range(0, kt, LANES):
            key0 = (t - 1) * kt + k0
            lo, hi = max(key0, 0), min(key0 + 2 * LANES, qt)
            parts = []
            if lo > 0:
                parts.append(jnp.full((LANES, lo), MASK_VALUE, F32))
            if hi > lo:
                parts.append(_bias_window(rb_ref, h, key0, lo, (LANES, hi - lo)))
            if max(hi, lo) < qt:
                parts.append(jnp.zeros((LANES, qt - max(hi, lo)), F32))
            strip = jnp.concatenate(parts, axis=1) if len(parts) > 1 else parts[0]
            for b in range(n_blocks):
                o_ref[0, t, b, k0:k0 + LANES, :] = strip[:, b * QUERY_BLOCK:(b + 1) * QUERY_BLOCK]


def _bias_tiles(rel_bias, kt):
    nb = 2 * kt // QUERY_BLOCK
    shape = (N_HEADS, 3, nb, kt, QUERY_BLOCK)
    return pl.pallas_call(
        functools.partial(_bias_kernel, kt=kt),
        grid=(N_HEADS,),
        in_specs=[pl.BlockSpec(memory_space=pltpu.SMEM)],
        out_specs=pl.BlockSpec((1,) + shape[1:], lambda h: (h, 0, 0, 0, 0)),
        out_shape=jax.ShapeDtypeStruct(shape, F32),
        compiler_params=_params("arbitrary"),
        name="rel_bias_tiles",
    )(rel_bias)


Q_BLOCK, K_BLOCK, V_BLOCK, GATE_B_BLOCK, GATE_C_BLOCK, H_BLOCK = range(6)


def _in_proj_kernel(x_ref, g_ref, w_ref, cw_ref, cg_ref, qk_ref, vt_ref, c_ref,
                    hn_ref, gb_ref, gc_ref, tail_ref, *, tiles_per_seq):
    i, j = pl.program_id(0), pl.program_id(1)
    chunks = _row_chunks(x_ref.shape[0])
    project = lambda hn: jnp.dot(hn, w_ref[...], preferred_element_type=F32)

    @pl.when(j == Q_BLOCK)
    def _():
        for rows in chunks:
            hn = (_rms(x_ref[rows, :], NORM_EPS) * g_ref[...]).astype(BF16)
            hn_ref[rows, :] = hn
            qk_ref[rows, :] = project(hn).astype(qk_ref.dtype)

    @pl.when(j == K_BLOCK)
    def _():
        qk_ref[...] = project(hn_ref[...]).astype(qk_ref.dtype)

    @pl.when(j == V_BLOCK)
    def _():
        for rows in chunks:
            vt_ref[:, rows] = project(hn_ref[rows, :]).T.astype(vt_ref.dtype)

    @pl.when(j == GATE_B_BLOCK)
    def _():
        gb_ref[...] = project(hn_ref[...])

    @pl.when(j == GATE_C_BLOCK)
    def _():
        gc_ref[...] = project(hn_ref[...])

    @pl.when(j == H_BLOCK)
    def _():
        above = jnp.where(i % tiles_per_seq == 0, 0.0, tail_ref[...])
        cw = cw_ref[...]
        row = lax.broadcasted_iota(jnp.int32, above.shape, 0)
        for rows in chunks:
            u = gc_ref[rows, :] * project(hn_ref[rows, :])
            u1, u2 = pltpu.roll(u, 1, 0), pltpu.roll(u, 2, 0)
            head1 = jnp.where(row == 0, above[7:8], u1[:SUBLANES])
            head2 = jnp.where(row == 0, above[6:7], jnp.where(row == 1, above[7:8], u2[:SUBLANES]))
            u1 = jnp.concatenate([head1, u1[SUBLANES:]], axis=0)
            u2 = jnp.concatenate([head2, u2[SUBLANES:]], axis=0)
            c = gb_ref[rows, :] * (cw[0:1] * u2 + cw[1:2] * u1 + cw[2:3] * u)
            c_ref[rows, :] = (_rms(c, NORM_EPS) * cg_ref[...]).astype(c_ref.dtype)
            above = u[-SUBLANES:]
        tail_ref[...] = above


def _in_proj(x, g, w, conv_w, conv_gain, *, tm, seq_len):
    m, d = x.shape
    tn = w.shape[1] // 6
    assert conv_w.shape[1] == tn and seq_len % tm == 0
    const = lambda shape: pl.BlockSpec(shape, lambda i, j: (0, 0))
    return pl.pallas_call(
        functools.partial(_in_proj_kernel, tiles_per_seq=seq_len // tm),
        grid=(m // tm, 6),
        in_specs=[
            pl.BlockSpec((tm, d), lambda i, j: (i, 0)),
            const((1, d)),
            pl.BlockSpec((d, tn), lambda i, j: (0, j)),
            const((CONV_K, tn)), const((1, tn)),
        ],
        out_specs=[
            pl.BlockSpec((tm, tn), lambda i, j: (i, jnp.minimum(j, K_BLOCK))),
            pl.BlockSpec((tn, tm), lambda i, j: (0, i)),
            pl.BlockSpec((tm, tn), lambda i, j: (i, 0)),
        ],
        out_shape=[
            jax.ShapeDtypeStruct((m, 2 * tn), BF16),
            jax.ShapeDtypeStruct((tn, m), BF16),
            jax.ShapeDtypeStruct((m, tn), BF16),
        ],
        scratch_shapes=[
            pltpu.VMEM((tm, d), BF16),
            pltpu.VMEM((tm, tn), F32),
            pltpu.VMEM((tm, tn), F32),
            pltpu.VMEM((SUBLANES, tn), F32),
        ],
        compiler_params=_params("arbitrary", "arbitrary"),
        name="norm_in_proj_conv",
    )(x, g, w, conv_w, conv_gain)


def _attn_q_tile(qi, q_ref, k_ref, vt_ref, bias_ref, lq1_ref, lk1_ref, lq2_ref, lk2_ref, g_ref,
                 o_ref, s_ref, m_ref, alpha_ref, acc_ref, *, kt, lambda_init):
    n_blocks = 2 * kt // QUERY_BLOCK
    all_q = range(n_blocks)
    block = lambda b: slice(b * QUERY_BLOCK, (b + 1) * QUERY_BLOCK)

    q = (q_ref[...].astype(F32) * (QK_DIM ** -0.5 * LOG2E)).astype(BF16)
    lane = lax.broadcasted_iota(jnp.int32, q.shape, 1)
    zero = jnp.zeros_like(q)
    q_halves = (jnp.where(lane < QK_DIM, q, zero), jnp.where(lane >= QK_DIM, q, zero))

    m_ref[...] = jnp.full(m_ref.shape, MASK_VALUE, F32)
    acc_ref[...] = jnp.zeros(acc_ref.shape, F32)

    def qk(t, b, nk):
        k_tile = k_ref[0, pl.ds(pl.multiple_of(t * kt, kt), nk), :]
        return [lax.dot_general(k_tile, q_halves[c][block(b)], (((1,), (1,)), ((), ())),
                                preferred_element_type=F32) for c in range(2)]

    def has_bias(bias_tile, b):
        if bias_tile is None:
            return False
        first_key = (bias_tile - 1) * kt - QUERY_BLOCK * b
        return first_key + kt - 1 > -MAX_DISTANCE

    def put_scores(scores, slot, bias_tile, b, nk):
        for c, s in enumerate(scores):
            if has_bias(bias_tile, b):
                s = s + bias_ref[0, bias_tile, b, :nk, :]
            m_prev = m_ref[1 - slot, c, b]
            m_new = jnp.maximum(m_prev, jnp.max(s, axis=0, keepdims=True))
            alpha_ref[slot, c, b] = jnp.exp2(m_prev - m_new)
            m_ref[slot, c, b] = m_new
            for h in range(QUERY_BLOCK // LANES):
                s_ref[slot, c, b, h, :nk, :] = s[:, h * LANES:(h + 1) * LANES]
                if nk < kt:
                    s_ref[slot, c, b, h, nk:, :] = jnp.full((kt - nk, LANES), MASK_VALUE, F32)

    def values_t(t):
        vt = vt_ref[:, pl.ds(pl.multiple_of(t * kt, kt), kt)]
        return jnp.concatenate([vt, jnp.ones((SUM_ROWS, kt), BF16)], axis=0)

    def accumulate(vt, slot, b, nk):
        vt = vt[:, :nk]
        for c in range(2):
            s = jnp.concatenate([s_ref[slot, c, b, h, :nk, :] for h in range(QUERY_BLOCK // LANES)],
                                axis=1)
            p = jnp.exp2(s - m_ref[slot, c, b])
            acc_ref[c, b] = alpha_ref[slot, c, b] * acc_ref[c, b] + jnp.dot(
                vt, p.astype(BF16), preferred_element_type=F32)

    def visible_keys(kind, b):
        if kind not in (1, 2):
            return kt
        return min(max(QUERY_BLOCK * (b + 1) - (kind - 1) * kt, 0), kt)

    def stage(nxt, slot, nxt_kind, cur, cur_kind):
        vt = values_t(cur)
        for b in all_q:
            scores = qk(nxt, b, visible_keys(nxt_kind, b))
            if visible_keys(cur_kind, b):
                accumulate(vt, 1 - slot, b, visible_keys(cur_kind, b))
            put_scores(scores, slot, nxt_kind, b, visible_keys(nxt_kind, b))

    for b in all_q:
        if visible_keys(2, b):
            put_scores(qk(2 * qi + 1, b, visible_keys(2, b)), 0, 2, b, visible_keys(2, b))
    stage(2 * qi, 1, 1, 2 * qi + 1, 2)

    @pl.when(qi >= 1)
    def _():
        stage(2 * qi - 1, 0, 0, 2 * qi, 1)
        stage(2 * qi - 2, 1, None, 2 * qi - 1, 0)

    def far_tiles(first, count, in_slot1):
        for t in range(0, count, 2):
            stage(first + t, 0, None, in_slot1, None)
            stage(first + t + 1, 1, None, first + t, None)
            in_slot1 = first + t + 1
        return in_slot1

    n_pairs = jnp.maximum(qi - 1, 0)
    in_slot1 = lax.fori_loop(0, n_pairs // 4, lambda p, in1: far_tiles(8 * p, 8, in1),
                             jnp.maximum(2 * qi - 2, 0))
    for pairs in (2, 1):
        first = 2 * (n_pairs - n_pairs % (2 * pairs))
        pending = n_pairs % (2 * pairs) >= pairs

        @pl.when(pending)
        def _():
            far_tiles(first, 2 * pairs, in_slot1)

        in_slot1 = jnp.where(pending, first + 2 * pairs - 1, in_slot1)

    lam = (jnp.exp(jnp.sum(lq1_ref[...] * lk1_ref[...], axis=-1, keepdims=True))
           - jnp.exp(jnp.sum(lq2_ref[...] * lk2_ref[...], axis=-1, keepdims=True))
           + lambda_init)
    vt = values_t(in_slot1)
    for b in all_q:
        accumulate(vt, 1, b, visible_keys(None, b))
        out = [acc_ref[c, b, :HEAD_DIM, :] / acc_ref[c, b, HEAD_DIM:HEAD_DIM + 1, :] for c in range(2)]
        a = out[0] - lam * out[1]
        a = a * lax.rsqrt(jnp.mean(a * a, axis=0, keepdims=True) + SUBLN_EPS)
        a = a * g_ref[...] * (1.0 - lambda_init)
        o_ref[block(b), :] = a.T.astype(o_ref.dtype)


def _attn_kernel(q_ref, k_ref, vt_ref, bias_ref, lq1_ref, lk1_ref, lq2_ref, lk2_ref, g_ref,
                 o_ref, *scratch, kt, lambda_init):
    qt = 2 * kt

    def q_tile(qi, carry):
        rows = pl.ds(pl.multiple_of(qi * qt, qt), qt)
        _attn_q_tile(qi, q_ref.at[0, rows, :], k_ref, vt_ref, bias_ref, lq1_ref, lk1_ref, lq2_ref,
                     lk2_ref, g_ref, o_ref.at[0, rows, :], *scratch, kt=kt, lambda_init=lambda_init)
        return carry

    lax.fori_loop(0, q_ref.shape[1] // qt, q_tile, 0)


def _attention(qk, vt, bias_tab, lq1, lk1, lq2, lk2, gain, *, lambda_init):
    b, s, _ = qk.shape
    kt = ATTN_K_TILE
    qt = 2 * kt
    nb = qt // QUERY_BLOCK
    vec = lambda n: pl.BlockSpec((1, n), lambda bi, h: (0, 0))
    return pl.pallas_call(
        functools.partial(_attn_kernel, kt=kt, lambda_init=lambda_init),
        grid=(b, N_HEADS),
        in_specs=[
            pl.BlockSpec((1, s, HEAD_DIM), lambda bi, h: (bi, 0, h)),
            pl.BlockSpec((1, s, HEAD_DIM), lambda bi, h: (bi, 0, N_HEADS + h)),
            pl.BlockSpec((HEAD_DIM, s), lambda bi, h: (h, bi)),
            pl.BlockSpec((1, 3, nb, kt, QUERY_BLOCK), lambda bi, h: (h, 0, 0, 0, 0)),
            vec(QK_DIM), vec(QK_DIM), vec(QK_DIM), vec(QK_DIM),
            pl.BlockSpec((HEAD_DIM, 1), lambda bi, h: (0, 0)),
        ],
        out_specs=pl.BlockSpec((1, s, HEAD_DIM), lambda bi, h: (bi, 0, h)),
        out_shape=jax.ShapeDtypeStruct((b, s, N_HEADS * HEAD_DIM), BF16),
        scratch_shapes=[
            pltpu.VMEM((2, 2, nb, QUERY_BLOCK // LANES, kt, LANES), F32),
            pltpu.VMEM((2, 2, nb, 1, QUERY_BLOCK), F32),
            pltpu.VMEM((2, 2, nb, 1, QUERY_BLOCK), F32),
            pltpu.VMEM((2, nb, HEAD_DIM + SUM_ROWS, QUERY_BLOCK), F32),
        ],
        compiler_params=_params("arbitrary", "arbitrary"),
        name="diff_attention",
    )(qk, qk, vt, bias_tab, lq1, lk1, lq2, lk2, gain)


def _out_proj_kernel(a_ref, c_ref, w_ref, x_ref, g_ref, o_ref):
    aw = a_ref.shape[1]
    for rows in _row_chunks(a_ref.shape[0], 2 * ROW_CHUNK):
        mix = jnp.dot(a_ref[rows, :], w_ref[:aw, :], preferred_element_type=F32)
        mix = mix + jnp.dot(c_ref[rows, :], w_ref[aw:, :], preferred_element_type=F32)
        o_ref[rows, :] = x_ref[rows, :] + _rms(mix, NORM_EPS) * g_ref[...]


def _out_proj(a, c, w, x, g, *, tm):
    m, d = x.shape
    row = lambda n: pl.BlockSpec((tm, n), lambda i: (i, 0))
    const = lambda shape: pl.BlockSpec(shape, lambda i: (0, 0))
    resident = pl.BlockSpec(w.shape, lambda i: (0, 0), pipeline_mode=pl.Buffered(1))
    return pl.pallas_call(
        _out_proj_kernel,
        grid=(m // tm,),
        in_specs=[row(a.shape[1]), row(c.shape[1]), resident, row(d), const((1, d))],
        out_specs=row(d),
        out_shape=jax.ShapeDtypeStruct((m, d), F32),
        compiler_params=_params("arbitrary"),
        name="out_proj",
    )(a, c, w, x, g)


def _ffn_kernel(x_ref, gpre_ref, wg_ref, wu_ref, wd_ref, gpost_ref, o_ref, hn_ref):
    j = pl.program_id(1)
    last = pl.num_programs(1) - 1
    chunks = _row_chunks(x_ref.shape[0], 2 * ROW_CHUNK)

    def partial_out(hn):
        gate = jnp.dot(hn, wg_ref[...], preferred_element_type=F32)
        up = jnp.dot(hn, wu_ref[...], preferred_element_type=F32)
        h = (gate * jax.nn.sigmoid(gate) * up).astype(BF16)
        return jnp.dot(h, wd_ref[...], preferred_element_type=F32)

    @pl.when(j == 0)
    def _():
        for rows in chunks:
            hn = (_rms(x_ref[rows, :], NORM_EPS) * gpre_ref[...]).astype(BF16)
            hn_ref[rows, :] = hn
            o_ref[rows, :] = partial_out(hn)

    @pl.when((j > 0) & (j < last))
    def _():
        o_ref[...] += partial_out(hn_ref[...])

    @pl.when(j == last)
    def _():
        for rows in chunks:
            f = o_ref[rows, :] + partial_out(hn_ref[rows, :])
            o_ref[rows, :] = x_ref[rows, :] + _rms(f, NORM_EPS) * gpost_ref[...]


def _ffn(x, gpre, wg, wu, wd, gpost, *, tm, tf):
    m, d = x.shape
    f = wg.shape[1]
    assert f // tf >= 2
    return pl.pallas_call(
        _ffn_kernel,
        grid=(m // tm, f // tf),
        in_specs=[
            pl.BlockSpec((tm, d), lambda i, j: (i, 0)),
            pl.BlockSpec((1, d), lambda i, j: (0, 0)),
            pl.BlockSpec((d, tf), lambda i, j: (0, j)),
            pl.BlockSpec((d, tf), lambda i, j: (0, j)),
            pl.BlockSpec((tf, d), lambda i, j: (j, 0)),
            pl.BlockSpec((1, d), lambda i, j: (0, 0)),
        ],
        out_specs=pl.BlockSpec((tm, d), lambda i, j: (i, 0)),
        out_shape=jax.ShapeDtypeStruct((m, d), F32),
        scratch_shapes=[pltpu.VMEM((tm, d), BF16)],
        compiler_params=_params("arbitrary", "arbitrary"),
        name="swiglu_ffn",
    )(x, gpre, wg, wu, wd, gpost)


def kernel(x, w_in, w_out, lambda_q1, lambda_k1, lambda_q2, lambda_k2, subln_gain, conv_w, conv_norm_gain, rel_bias, w_gate, w_up, w_down, norm_mix_pre, norm_mix_post, norm_ffn_pre, norm_ffn_post):
    b, s, d = x.shape
    depth = w_in.shape[0]
    assert s % (2 * ATTN_K_TILE) == 0 and ATTN_K_TILE >= MAX_DISTANCE
    assert s % ROW_TILE == 0 and w_gate.shape[2] % FFN_CHUNK == 0
    assert w_in.shape[2] == 6 * N_HEADS * HEAD_DIM and 2 * QK_DIM == HEAD_DIM

    w_in, w_out, w_gate, w_up, w_down = (w.astype(BF16) for w in (w_in, w_out, w_gate, w_up, w_down))
    bias_tab = _bias_tiles(rel_bias, ATTN_K_TILE)
    row = lambda v: v.reshape(1, -1)

    xf = x.reshape(b * s, d)
    for l in range(depth):
        lambda_init = 0.8 - 0.6 * math.exp(-0.3 * l)
        qk, vt, c = _in_proj(xf, row(norm_mix_pre[l]), w_in[l], conv_w[l], row(conv_norm_gain[l]),
                             tm=ROW_TILE, seq_len=s)
        a = _attention(qk.reshape(b, s, -1), vt, bias_tab, row(lambda_q1[l]), row(lambda_k1[l]),
                       row(lambda_q2[l]), row(lambda_k2[l]), subln_gain[l].reshape(-1, 1),
                       lambda_init=lambda_init)
        xf = _out_proj(a.reshape(b * s, -1), c, w_out[l], xf, row(norm_mix_post[l]), tm=ROW_TILE)
        xf = _ffn(xf, row(norm_ffn_pre[l]), w_gate[l], w_up[l], w_down[l], row(norm_ffn_post[l]),
                  tm=ROW_TILE, tf=FFN_CHUNK)
    return xf.reshape(b, s, d)
```

```python
import functools
import math

import jax
import jax.numpy as jnp
from jax import lax
from jax.experimental import pallas as pl
from jax.experimental.pallas import tpu as pltpu

N_HEADS = 8
HEAD_DIM = 128
QK_DIM = 64
CONV_K = 3
N_BUCKETS = 32
MAX_DISTANCE = 128
NORM_EPS = 1e-6
SUBLN_EPS = 1e-5

LANES = 128
SUBLANES = 8
VMEM_LIMIT = 56 * 1024 * 1024

ROW_TILE = 1024
FFN_CHUNK = 512
ROW_CHUNK = 256
ATTN_K_TILE = 512
SUM_ROWS = 16
QUERY_BLOCK = 256
MASK_VALUE = -1e30
LOG2E = math.log2(math.e)

F32 = jnp.float32
BF16 = jnp.bfloat16


def _params(*sem):
    return pltpu.CompilerParams(dimension_semantics=sem, vmem_limit_bytes=VMEM_LIMIT)


def _rms(x, eps):
    return x * lax.rsqrt(jnp.mean(x * x, axis=-1, keepdims=True) + eps)


def _row_chunks(rows, chunk=ROW_CHUNK):
    assert rows % chunk == 0
    return [slice(r, r + chunk) for r in range(0, rows, chunk)]


def _bias_window(rb_ref, h, key0, query0, shape):
    dist = (query0 - key0 + lax.broadcasted_iota(jnp.int32, shape, 1)
            - lax.broadcasted_iota(jnp.int32, shape, 0))
    n = jnp.maximum(dist, 0)
    max_exact = N_BUCKETS // 2
    nf = jnp.maximum(n, max_exact).astype(F32)
    large = max_exact + (jnp.log(nf / max_exact) / math.log(MAX_DISTANCE / max_exact)
                         * (N_BUCKETS - max_exact)).astype(jnp.int32)
    large = jnp.minimum(large, N_BUCKETS - 1)
    bucket = jnp.where(n < max_exact, n, large)
    val = jnp.zeros(shape, F32)
    for b in range(N_BUCKETS):
        val = jnp.where(bucket == b, rb_ref[b, h], val)
    val = (val - rb_ref[N_BUCKETS - 1, h]) * LOG2E
    return jnp.where(dist >= 0, val, MASK_VALUE)


def _bias_kernel(rb_ref, o_ref, *, kt):
    h = pl.program_id(0)
    _, n_tiles, n_blocks, _, _ = o_ref.shape
    qt = n_blocks * QUERY_BLOCK
    assert MAX_DISTANCE <= LANES
    for t in range(n_tiles):
        for k0 in range(0, kt, LANES):
            key0 = (t - 1) * kt + k0
            lo, hi = max(key0, 0), min(key0 + 2 * LANES, qt)
            parts = []
            if lo > 0:
                parts.append(jnp.full((LANES, lo), MASK_VALUE, F32))
            if hi > lo:
                parts.append(_bias_window(rb_ref, h, key0, lo, (LANES, hi - lo)))
            if max(hi, lo) < qt:
                parts.append(jnp.zeros((LANES, qt - max(hi, lo)), F32))
            strip = jnp.concatenate(parts, axis=1) if len(parts) > 1 else parts[0]
            for b in range(n_blocks):
                o_ref[0, t, b, k0:k0 + LANES, :] = strip[:, b * QUERY_BLOCK:(b + 1) * QUERY_BLOCK]


def _bias_tiles(rel_bias, kt):
    nb = 2 * kt // QUERY_BLOCK
    shape = (N_HEADS, 3, nb, kt, QUERY_BLOCK)
    return pl.pallas_call(
        functools.partial(_bias_kernel, kt=kt),
        grid=(N_HEADS,),
        in_specs=[pl.BlockSpec(memory_space=pltpu.SMEM)],
        out_specs=pl.BlockSpec((1,) + shape[1:], lambda h: (h, 0, 0, 0, 0)),
        out_shape=jax.ShapeDtypeStruct(shape, F32),
        compiler_params=_params("arbitrary"),
        name="rel_bias_tiles",
    )(rel_bias)


Q_BLOCK, K_BLOCK, V_BLOCK, GATE_B_BLOCK, GATE_C_BLOCK, H_BLOCK = range(6)


def _in_proj_kernel(x_ref, g_ref, w_ref, cw_ref, cg_ref, qk_ref, vt_ref, c_ref,
                    hn_ref, gb_ref, gc_ref, tail_ref, *, tiles_per_seq):
    i, j = pl.program_id(0), pl.program_id(1)
    chunks = _row_chunks(x_ref.shape[0])
    project = lambda hn: jnp.dot(hn, w_ref[...], preferred_element_type=F32)

    @pl.when(j == Q_BLOCK)
    def _():
        for rows in chunks:
            hn = (_rms(x_ref[rows, :], NORM_EPS) * g_ref[...]).astype(BF16)
            hn_ref[rows, :] = hn
            qk_ref[rows, :] = project(hn).astype(qk_ref.dtype)

    @pl.when(j == K_BLOCK)
    def _():
        qk_ref[...] = project(hn_ref[...]).astype(qk_ref.dtype)

    @pl.when(j == V_BLOCK)
    def _():
        for rows in chunks:
            vt_ref[:, rows] = project(hn_ref[rows, :]).T.astype(vt_ref.dtype)

    @pl.when(j == GATE_B_BLOCK)
    def _():
        gb_ref[...] = project(hn_ref[...])

    @pl.when(j == GATE_C_BLOCK)
    def _():
        gc_ref[...] = project(hn_ref[...])

    @pl.when(j == H_BLOCK)
    def _():
        above = jnp.where(i % tiles_per_seq == 0, 0.0, tail_ref[...])
        cw = cw_ref[...]
        row = lax.broadcasted_iota(jnp.int32, above.shape, 0)
        for rows in chunks:
            u = gc_ref[rows, :] * project(hn_ref[rows, :])
            u1, u2 = pltpu.roll(u, 1, 0), pltpu.roll(u, 2, 0)
            head1 = jnp.where(row == 0, above[7:8], u1[:SUBLANES])
            head2 = jnp.where(row == 0, above[6:7], jnp.where(row == 1, above[7:8], u2[:SUBLANES]))
            u1 = jnp.concatenate([head1, u1[SUBLANES:]], axis=0)
            u2 = jnp.concatenate([head2, u2[SUBLANES:]], axis=0)
            c = gb_ref[rows, :] * (cw[0:1] * u2 + cw[1:2] * u1 + cw[2:3] * u)
            c_ref[rows, :] = (_rms(c, NORM_EPS) * cg_ref[...]).astype(c_ref.dtype)
            above = u[-SUBLANES:]
        tail_ref[...] = above


def _in_proj(x, g, w, layer, conv_w, conv_gain, *, tm, seq_len):
    m, d = x.shape
    tn = w.shape[2] // 6
    assert conv_w.shape[1] == tn and seq_len % tm == 0
    const = lambda shape: pl.BlockSpec(shape, lambda i, j: (0, 0))
    return pl.pallas_call(
        functools.partial(_in_proj_kernel, tiles_per_seq=seq_len // tm),
        grid=(m // tm, 6),
        in_specs=[
            pl.BlockSpec((tm, d), lambda i, j: (i, 0)),
            const((1, d)),
            pl.BlockSpec((None, d, tn), lambda i, j: (layer, 0, j)),
            const((CONV_K, tn)), const((1, tn)),
        ],
        out_specs=[
            pl.BlockSpec((tm, tn), lambda i, j: (i, jnp.minimum(j, K_BLOCK))),
            pl.BlockSpec((tn, tm), lambda i, j: (0, i)),
            pl.BlockSpec((tm, tn), lambda i, j: (i, 0)),
        ],
        out_shape=[
            jax.ShapeDtypeStruct((m, 2 * tn), BF16),
            jax.ShapeDtypeStruct((tn, m), BF16),
            jax.ShapeDtypeStruct((m, tn), BF16),
        ],
        scratch_shapes=[
            pltpu.VMEM((tm, d), BF16),
            pltpu.VMEM((tm, tn), F32),
            pltpu.VMEM((tm, tn), F32),
            pltpu.VMEM((SUBLANES, tn), F32),
        ],
        compiler_params=_params("arbitrary", "arbitrary"),
        name="norm_in_proj_conv",
    )(x, g, w, conv_w, conv_gain)


def _attn_kernel(q_ref, k_ref, vt_ref, bias_ref, lq1_ref, lk1_ref, lq2_ref, lk2_ref, g_ref,
                 o_ref, s_ref, m_ref, alpha_ref, acc_ref, *, kt, lambda_init):
    qi = pl.program_id(2)
    n_blocks = 2 * kt // QUERY_BLOCK
    all_q = range(n_blocks)
    block = lambda b: slice(b * QUERY_BLOCK, (b + 1) * QUERY_BLOCK)

    q = (q_ref[0].astype(F32) * (QK_DIM ** -0.5 * LOG2E)).astype(BF16)
    lane = lax.broadcasted_iota(jnp.int32, q.shape, 1)
    zero = jnp.zeros_like(q)
    q_halves = (jnp.where(lane < QK_DIM, q, zero), jnp.where(lane >= QK_DIM, q, zero))

    m_ref[...] = jnp.full(m_ref.shape, MASK_VALUE, F32)
    acc_ref[...] = jnp.zeros(acc_ref.shape, F32)

    def qk(t, b, nk):
        k_tile = k_ref[0, pl.ds(pl.multiple_of(t * kt, kt), nk), :]
        return [lax.dot_general(k_tile, q_halves[c][block(b)], (((1,), (1,)), ((), ())),
                                preferred_element_type=F32) for c in range(2)]

    def has_bias(bias_tile, b):
        if bias_tile is None:
            return False
        first_key = (bias_tile - 1) * kt - QUERY_BLOCK * b
        return first_key + kt - 1 > -MAX_DISTANCE

    def put_scores(scores, slot, bias_tile, b, nk):
        for c, s in enumerate(scores):
            if has_bias(bias_tile, b):
                s = s + bias_ref[0, bias_tile, b, :nk, :]
            m_prev = m_ref[1 - slot, c, b]
            m_new = jnp.maximum(m_prev, jnp.max(s, axis=0, keepdims=True))
            alpha_ref[slot, c, b] = jnp.exp2(m_prev - m_new)
            m_ref[slot, c, b] = m_new
            for h in range(QUERY_BLOCK // LANES):
                s_ref[slot, c, b, h, :nk, :] = s[:, h * LANES:(h + 1) * LANES]
                if nk < kt:
                    s_ref[slot, c, b, h, nk:, :] = jnp.full((kt - nk, LANES), MASK_VALUE, F32)

    def values_t(t):
        vt = vt_ref[:, pl.ds(pl.multiple_of(t * kt, kt), kt)]
        return jnp.concatenate([vt, jnp.ones((SUM_ROWS, kt), BF16)], axis=0)

    def accumulate(vt, slot, b, nk):
        vt = vt[:, :nk]
        for c in range(2):
            s = jnp.concatenate([s_ref[slot, c, b, h, :nk, :] for h in range(QUERY_BLOCK // LANES)],
                                axis=1)
            p = jnp.exp2(s - m_ref[slot, c, b])
            acc_ref[c, b] = alpha_ref[slot, c, b] * acc_ref[c, b] + jnp.dot(
                vt, p.astype(BF16), preferred_element_type=F32)

    def visible_keys(kind, b):
        if kind not in (1, 2):
            return kt
        return min(max(QUERY_BLOCK * (b + 1) - (kind - 1) * kt, 0), kt)

    def stage(nxt, slot, nxt_kind, cur, cur_kind):
        vt = values_t(cur)
        for b in all_q:
            scores = qk(nxt, b, visible_keys(nxt_kind, b))
            if visible_keys(cur_kind, b):
                accumulate(vt, 1 - slot, b, visible_keys(cur_kind, b))
            put_scores(scores, slot, nxt_kind, b, visible_keys(nxt_kind, b))

    for b in all_q:
        if visible_keys(2, b):
            put_scores(qk(2 * qi + 1, b, visible_keys(2, b)), 0, 2, b, visible_keys(2, b))
    stage(2 * qi, 1, 1, 2 * qi + 1, 2)

    @pl.when(qi >= 1)
    def _():
        stage(2 * qi - 1, 0, 0, 2 * qi, 1)
        stage(2 * qi - 2, 1, None, 2 * qi - 1, 0)

    def far_tiles(first, count, in_slot1):
        for t in range(0, count, 2):
            stage(first + t, 0, None, in_slot1, None)
            stage(first + t + 1, 1, None, first + t, None)
            in_slot1 = first + t + 1
        return in_slot1

    n_pairs = jnp.maximum(qi - 1, 0)
    in_slot1 = lax.fori_loop(0, n_pairs // 4, lambda p, in1: far_tiles(8 * p, 8, in1),
                             jnp.maximum(2 * qi - 2, 0))
    for pairs in (2, 1):
        first = 2 * (n_pairs - n_pairs % (2 * pairs))
        pending = n_pairs % (2 * pairs) >= pairs

        @pl.when(pending)
        def _():
            far_tiles(first, 2 * pairs, in_slot1)

        in_slot1 = jnp.where(pending, first + 2 * pairs - 1, in_slot1)

    lam = (jnp.exp(jnp.sum(lq1_ref[...] * lk1_ref[...], axis=-1, keepdims=True))
           - jnp.exp(jnp.sum(lq2_ref[...] * lk2_ref[...], axis=-1, keepdims=True))
           + lambda_init)
    vt = values_t(in_slot1)
    for b in all_q:
        accumulate(vt, 1, b, visible_keys(None, b))
        out = [acc_ref[c, b, :HEAD_DIM, :] / acc_ref[c, b, HEAD_DIM:HEAD_DIM + 1, :] for c in range(2)]
        a = out[0] - lam * out[1]
        a = a * lax.rsqrt(jnp.mean(a * a, axis=0, keepdims=True) + SUBLN_EPS)
        a = a * g_ref[...] * (1.0 - lambda_init)
        o_ref[0, block(b), :] = a.T.astype(o_ref.dtype)


def _attention(qk, vt, bias_tab, lq1, lk1, lq2, lk2, gain, *, lambda_init):
    b, s, _ = qk.shape
    kt = ATTN_K_TILE
    qt = 2 * kt
    nb = qt // QUERY_BLOCK
    vec = lambda n: pl.BlockSpec((1, n), lambda bi, h, qi: (0, 0))
    return pl.pallas_call(
        functools.partial(_attn_kernel, kt=kt, lambda_init=lambda_init),
        grid=(b, N_HEADS, s // qt),
        in_specs=[
            pl.BlockSpec((1, qt, HEAD_DIM), lambda bi, h, qi: (bi, qi, h)),
            pl.BlockSpec((1, s, HEAD_DIM), lambda bi, h, qi: (bi, 0, N_HEADS + h)),
            pl.BlockSpec((HEAD_DIM, s), lambda bi, h, qi: (h, bi)),
            pl.BlockSpec((1, 3, nb, kt, QUERY_BLOCK), lambda bi, h, qi: (h, 0, 0, 0, 0)),
            vec(QK_DIM), vec(QK_DIM), vec(QK_DIM), vec(QK_DIM),
            pl.BlockSpec((HEAD_DIM, 1), lambda bi, h, qi: (0, 0)),
        ],
        out_specs=pl.BlockSpec((1, qt, HEAD_DIM), lambda bi, h, qi: (bi, qi, h)),
        out_shape=jax.ShapeDtypeStruct((b, s, N_HEADS * HEAD_DIM), BF16),
        scratch_shapes=[
            pltpu.VMEM((2, 2, nb, QUERY_BLOCK // LANES, kt, LANES), F32),
            pltpu.VMEM((2, 2, nb, 1, QUERY_BLOCK), F32),
            pltpu.VMEM((2, 2, nb, 1, QUERY_BLOCK), F32),
            pltpu.VMEM((2, nb, HEAD_DIM + SUM_ROWS, QUERY_BLOCK), F32),
        ],
        compiler_params=_params("arbitrary", "arbitrary", "arbitrary"),
        name="diff_attention",
    )(qk, qk, vt, bias_tab, lq1, lk1, lq2, lk2, gain)


def _out_proj_kernel(a_ref, c_ref, w_ref, x_ref, g_ref, o_ref):
    aw = a_ref.shape[1]
    for rows in _row_chunks(a_ref.shape[0], 2 * ROW_CHUNK):
        mix = jnp.dot(a_ref[rows, :], w_ref[:aw, :], preferred_element_type=F32)
        mix = mix + jnp.dot(c_ref[rows, :], w_ref[aw:, :], preferred_element_type=F32)
        o_ref[rows, :] = x_ref[rows, :] + _rms(mix, NORM_EPS) * g_ref[...]


def _out_proj(a, c, w, layer, x, g, *, tm):
    m, d = x.shape
    row = lambda n: pl.BlockSpec((tm, n), lambda i: (i, 0))
    const = lambda shape: pl.BlockSpec(shape, lambda i: (0, 0))
    resident = pl.BlockSpec((None,) + w.shape[1:], lambda i: (layer, 0, 0), pipeline_mode=pl.Buffered(1))
    return pl.pallas_call(
        _out_proj_kernel,
        grid=(m // tm,),
        in_specs=[row(a.shape[1]), row(c.shape[1]), resident, row(d), const((1, d))],
        out_specs=row(d),
        out_shape=jax.ShapeDtypeStruct((m, d), F32),
        compiler_params=_params("arbitrary"),
        name="out_proj",
    )(a, c, w, x, g)


def _ffn_kernel(x_ref, gpre_ref, wg_ref, wu_ref, wd_ref, gpost_ref, o_ref, hn_ref):
    j = pl.program_id(1)
    last = pl.num_programs(1) - 1
    chunks = _row_chunks(x_ref.shape[0], 2 * ROW_CHUNK)

    def partial_out(hn):
        gate = jnp.dot(hn, wg_ref[...], preferred_element_type=F32)
        up = jnp.dot(hn, wu_ref[...], preferred_element_type=F32)
        h = (gate * jax.nn.sigmoid(gate) * up).astype(BF16)
        return jnp.dot(h, wd_ref[...], preferred_element_type=F32)

    @pl.when(j == 0)
    def _():
        for rows in chunks:
            hn = (_rms(x_ref[rows, :], NORM_EPS) * gpre_ref[...]).astype(BF16)
            hn_ref[rows, :] = hn
            o_ref[rows, :] = partial_out(hn)

    @pl.when((j > 0) & (j < last))
    def _():
        o_ref[...] += partial_out(hn_ref[...])

    @pl.when(j == last)
    def _():
        for rows in chunks:
            f = o_ref[rows, :] + partial_out(hn_ref[rows, :])
            o_ref[rows, :] = x_ref[rows, :] + _rms(f, NORM_EPS) * gpost_ref[...]


def _ffn(x, gpre, wg, wu, wd, layer, gpost, *, tm, tf):
    m, d = x.shape
    f = wg.shape[2]
    assert f // tf >= 2
    return pl.pallas_call(
        _ffn_kernel,
        grid=(m // tm, f // tf),
        in_specs=[
            pl.BlockSpec((tm, d), lambda i, j: (i, 0)),
            pl.BlockSpec((1, d), lambda i, j: (0, 0)),
            pl.BlockSpec((None, d, tf), lambda i, j: (layer, 0, j)),
            pl.BlockSpec((None, d, tf), lambda i, j: (layer, 0, j)),
            pl.BlockSpec((None, tf, d), lambda i, j: (layer, j, 0)),
            pl.BlockSpec((1, d), lambda i, j: (0, 0)),
        ],
        out_specs=pl.BlockSpec((tm, d), lambda i, j: (i, 0)),
        out_shape=jax.ShapeDtypeStruct((m, d), F32),
        scratch_shapes=[pltpu.VMEM((tm, d), BF16)],
        compiler_params=_params("arbitrary", "arbitrary"),
        name="swiglu_ffn",
    )(x, gpre, wg, wu, wd, gpost)


def kernel(x, w_in, w_out, lambda_q1, lambda_k1, lambda_q2, lambda_k2, subln_gain, conv_w, conv_norm_gain, rel_bias, w_gate, w_up, w_down, norm_mix_pre, norm_mix_post, norm_ffn_pre, norm_ffn_post):
    b, s, d = x.shape
    depth = w_in.shape[0]
    assert s % (2 * ATTN_K_TILE) == 0 and ATTN_K_TILE >= MAX_DISTANCE
    assert s % ROW_TILE == 0 and w_gate.shape[2] % FFN_CHUNK == 0
    assert w_in.shape[2] == 6 * N_HEADS * HEAD_DIM and 2 * QK_DIM == HEAD_DIM

    w_in, w_out, w_gate, w_up, w_down = (w.astype(BF16) for w in (w_in, w_out, w_gate, w_up, w_down))
    bias_tab = _bias_tiles(rel_bias, ATTN_K_TILE)
    row = lambda v: v.reshape(1, -1)

    xf = x.reshape(b * s, d)
    for l in range(depth):
        lambda_init = 0.8 - 0.6 * math.exp(-0.3 * l)
        qk, vt, c = _in_proj(xf, row(norm_mix_pre[l]), w_in, l, conv_w[l], row(conv_norm_gain[l]),
                             tm=ROW_TILE, seq_len=s)
        a = _attention(qk.reshape(b, s, -1), vt, bias_tab, row(lambda_q1[l]), row(lambda_k1[l]),
                       row(lambda_q2[l]), row(lambda_k2[l]), subln_gain[l].reshape(-1, 1),
                       lambda_init=lambda_init)
        xf = _out_proj(a.reshape(b * s, -1), c, w_out, l, xf, row(norm_mix_post[l]), tm=ROW_TILE)
        xf = _ffn(xf, row(norm_ffn_pre[l]), w_gate, w_up, w_down, l, row(norm_ffn_post[l]),
                  tm=ROW_TILE, tf=FFN_CHUNK)
    return xf.reshape(b, s, d)
```

```python
import functools
import math

import jax
import jax.numpy as jnp
from jax import lax
from jax.experimental import pallas as pl
from jax.experimental.pallas import tpu as pltpu

N_HEADS = 8
HEAD_DIM = 128
QK_DIM = 64
CONV_K = 3
N_BUCKETS = 32
MAX_DISTANCE = 128
NORM_EPS = 1e-6
SUBLN_EPS = 1e-5

LANES = 128
SUBLANES = 8
VMEM_LIMIT = 56 * 1024 * 1024

ROW_TILE = 1024
FFN_CHUNK = 512
ROW_CHUNK = 256
ATTN_K_TILE = 512
SUM_ROWS = 16
QUERY_BLOCK = 256
MASK_VALUE = -1e30
LOG2E = math.log2(math.e)

F32 = jnp.float32
BF16 = jnp.bfloat16


def _params(*sem):
    return pltpu.CompilerParams(dimension_semantics=sem, vmem_limit_bytes=VMEM_LIMIT)


def _rms(x, eps):
    return x * lax.rsqrt(jnp.mean(x * x, axis=-1, keepdims=True) + eps)


def _row_chunks(rows, chunk=ROW_CHUNK):
    assert rows % chunk == 0
    return [slice(r, r + chunk) for r in range(0, rows, chunk)]


def _bias_window(rb_ref, h, key0, query0, shape):
    dist = (query0 - key0 + lax.broadcasted_iota(jnp.int32, shape, 1)
            - lax.broadcasted_iota(jnp.int32, shape, 0))
    n = jnp.maximum(dist, 0)
    max_exact = N_BUCKETS // 2
    nf = jnp.maximum(n, max_exact).astype(F32)
    large = max_exact + (jnp.log(nf / max_exact) / math.log(MAX_DISTANCE / max_exact)
                         * (N_BUCKETS - max_exact)).astype(jnp.int32)
    large = jnp.minimum(large, N_BUCKETS - 1)
    bucket = jnp.where(n < max_exact, n, large)
    val = jnp.zeros(shape, F32)
    for b in range(N_BUCKETS):
        val = jnp.where(bucket == b, rb_ref[b, h], val)
    val = (val - rb_ref[N_BUCKETS - 1, h]) * LOG2E
    return jnp.where(dist >= 0, val, MASK_VALUE)


def _bias_kernel(rb_ref, o_ref, *, kt):
    h = pl.program_id(0)
    _, n_tiles, n_blocks, _, _ = o_ref.shape
    qt = n_blocks * QUERY_BLOCK
    assert MAX_DISTANCE <= LANES
    for t in range(n_tiles):
        for k0 in range(0, kt, LANES):
            key0 = (t - 1) * kt + k0
            lo, hi = max(key0, 0), min(key0 + 2 * LANES, qt)
            parts = []
            if lo > 0:
                parts.append(jnp.full((LANES, lo), MASK_VALUE, F32))
            if hi > lo:
                parts.append(_bias_window(rb_ref, h, key0, lo, (LANES, hi - lo)))
            if max(hi, lo) < qt:
                parts.append(jnp.zeros((LANES, qt - max(hi, lo)), F32))
            strip = jnp.concatenate(parts, axis=1) if len(parts) > 1 else parts[0]
            for b in range(n_blocks):
                o_ref[0, t, b, k0:k0 + LANES, :] = strip[:, b * QUERY_BLOCK:(b + 1) * QUERY_BLOCK]


def _bias_tiles(rel_bias, kt):
    nb = 2 * kt // QUERY_BLOCK
    shape = (N_HEADS, 3, nb, kt, QUERY_BLOCK)
    return pl.pallas_call(
        functools.partial(_bias_kernel, kt=kt),
        grid=(N_HEADS,),
        in_specs=[pl.BlockSpec(memory_space=pltpu.SMEM)],
        out_specs=pl.BlockSpec((1,) + shape[1:], lambda h: (h, 0, 0, 0, 0)),
        out_shape=jax.ShapeDtypeStruct(shape, F32),
        compiler_params=_params("arbitrary"),
        name="rel_bias_tiles",
    )(rel_bias)


Q_BLOCK, K_BLOCK, V_BLOCK, GATE_B_BLOCK, GATE_C_BLOCK, H_BLOCK = range(6)


def _in_proj_kernel(x_ref, g_ref, w_ref, cw_ref, cg_ref, qk_ref, vt_ref, c_ref,
                    hn_ref, gb_ref, gc_ref, tail_ref, *, tiles_per_seq):
    i, j = pl.program_id(0), pl.program_id(1)
    chunks = _row_chunks(x_ref.shape[0])
    project = lambda hn: jnp.dot(hn, w_ref[...], preferred_element_type=F32)

    @pl.when(j == Q_BLOCK)
    def _():
        for rows in chunks:
            hn = (_rms(x_ref[rows, :], NORM_EPS) * g_ref[...]).astype(BF16)
            hn_ref[rows, :] = hn
            qk_ref[rows, :] = project(hn).astype(qk_ref.dtype)

    @pl.when(j == K_BLOCK)
    def _():
        qk_ref[...] = project(hn_ref[...]).astype(qk_ref.dtype)

    @pl.when(j == V_BLOCK)
    def _():
        for rows in chunks:
            vt_ref[:, rows] = project(hn_ref[rows, :]).T.astype(vt_ref.dtype)

    @pl.when(j == GATE_B_BLOCK)
    def _():
        gb_ref[...] = project(hn_ref[...])

    @pl.when(j == GATE_C_BLOCK)
    def _():
        gc_ref[...] = project(hn_ref[...])

    @pl.when(j == H_BLOCK)
    def _():
        above = jnp.where(i % tiles_per_seq == 0, 0.0, tail_ref[...])
        cw = cw_ref[...]
        row = lax.broadcasted_iota(jnp.int32, above.shape, 0)
        for rows in chunks:
            u = gc_ref[rows, :] * project(hn_ref[rows, :])
            u1, u2 = pltpu.roll(u, 1, 0), pltpu.roll(u, 2, 0)
            head1 = jnp.where(row == 0, above[7:8], u1[:SUBLANES])
            head2 = jnp.where(row == 0, above[6:7], jnp.where(row == 1, above[7:8], u2[:SUBLANES]))
            u1 = jnp.concatenate([head1, u1[SUBLANES:]], axis=0)
            u2 = jnp.concatenate([head2, u2[SUBLANES:]], axis=0)
            c = gb_ref[rows, :] * (cw[0:1] * u2 + cw[1:2] * u1 + cw[2:3] * u)
            c_ref[rows, :] = (_rms(c, NORM_EPS) * cg_ref[...]).astype(c_ref.dtype)
            above = u[-SUBLANES:]
        tail_ref[...] = above


def _in_proj(x, g, w, layer, conv_w, conv_gain, *, tm, seq_len):
    m, d = x.shape
    tn = w.shape[2] // 6
    assert conv_w.shape[1] == tn and seq_len % tm == 0
    const = lambda shape: pl.BlockSpec(shape, lambda i, j: (0, 0))
    return pl.pallas_call(
        functools.partial(_in_proj_kernel, tiles_per_seq=seq_len // tm),
        grid=(m // tm, 6),
        in_specs=[
            pl.BlockSpec((tm, d), lambda i, j: (i, 0)),
            const((1, d)),
            pl.BlockSpec((None, d, tn), lambda i, j: (layer, 0, j)),
            const((CONV_K, tn)), const((1, tn)),
        ],
        out_specs=[
            pl.BlockSpec((tm, tn), lambda i, j: (i, jnp.minimum(j, K_BLOCK))),
            pl.BlockSpec((tn, tm), lambda i, j: (0, i)),
            pl.BlockSpec((tm, tn), lambda i, j: (i, 0)),
        ],
        out_shape=[
            jax.ShapeDtypeStruct((m, 2 * tn), BF16),
            jax.ShapeDtypeStruct((tn, m), BF16),
            jax.ShapeDtypeStruct((m, tn), BF16),
        ],
        scratch_shapes=[
            pltpu.VMEM((tm, d), BF16),
            pltpu.VMEM((tm, tn), F32),
            pltpu.VMEM((tm, tn), F32),
            pltpu.VMEM((SUBLANES, tn), F32),
        ],
        compiler_params=_params("arbitrary", "arbitrary"),
        name="norm_in_proj_conv",
    )(x, g, w, conv_w, conv_gain)


def _attn_kernel(q_ref, k_ref, vt_ref, bias_ref, lq1_ref, lk1_ref, lq2_ref, lk2_ref, g_ref,
                 o_ref, s_ref, m_ref, alpha_ref, acc_ref, *, kt, lambda_init):
    qi = pl.program_id(2)
    n_blocks = 2 * kt // QUERY_BLOCK
    all_q = range(n_blocks)
    block = lambda b: slice(b * QUERY_BLOCK, (b + 1) * QUERY_BLOCK)

    q = (q_ref[0].astype(F32) * (QK_DIM ** -0.5 * LOG2E)).astype(BF16)
    lane = lax.broadcasted_iota(jnp.int32, q.shape, 1)
    zero = jnp.zeros_like(q)
    q_halves = (jnp.where(lane < QK_DIM, q, zero), jnp.where(lane >= QK_DIM, q, zero))

    m_ref[...] = jnp.full(m_ref.shape, MASK_VALUE, F32)
    acc_ref[...] = jnp.zeros(acc_ref.shape, F32)

    def qk(t, b, nk):
        k_tile = k_ref[0, pl.ds(pl.multiple_of(t * kt, kt), nk), :]
        return [lax.dot_general(k_tile, q_halves[c][block(b)], (((1,), (1,)), ((), ())),
                                preferred_element_type=F32) for c in range(2)]

    def has_bias(bias_tile, b):
        if bias_tile is None:
            return False
        first_key = (bias_tile - 1) * kt - QUERY_BLOCK * b
        return first_key + kt - 1 > -MAX_DISTANCE

    def put_scores(scores, slot, bias_tile, b, nk):
        for c, s in enumerate(scores):
            if has_bias(bias_tile, b):
                s = s + bias_ref[0, bias_tile, b, :nk, :]
            m_prev = m_ref[1 - slot, c, b]
            m_new = jnp.maximum(m_prev, jnp.max(s, axis=0, keepdims=True))
            alpha_ref[slot, c, b] = jnp.exp2(m_prev - m_new)
            m_ref[slot, c, b] = m_new
            for h in range(QUERY_BLOCK // LANES):
                s_ref[slot, c, b, h, :nk, :] = s[:, h * LANES:(h + 1) * LANES]
                if nk < kt:
                    s_ref[slot, c, b, h, nk:, :] = jnp.full((kt - nk, LANES), MASK_VALUE, F32)

    def values_t(t):
        vt = vt_ref[:, pl.ds(pl.multiple_of(t * kt, kt), kt)]
        return jnp.concatenate([vt, jnp.ones((SUM_ROWS, kt), BF16)], axis=0)

    def accumulate(vt, slot, b, nk):
        vt = vt[:, :nk]
        for c in range(2):
            s = jnp.concatenate([s_ref[slot, c, b, h, :nk, :] for h in range(QUERY_BLOCK // LANES)],
                                axis=1)
            p = jnp.exp2(s - m_ref[slot, c, b])
            acc_ref[c, b] = alpha_ref[slot, c, b] * acc_ref[c, b] + jnp.dot(
                vt, p.astype(BF16), preferred_element_type=F32)

    def visible_keys(kind, b):
        if kind not in (1, 2):
            return kt
        return min(max(QUERY_BLOCK * (b + 1) - (kind - 1) * kt, 0), kt)

    def stage(nxt, slot, nxt_kind, cur, cur_kind):
        vt = values_t(cur)
        for b in all_q:
            scores = qk(nxt, b, visible_keys(nxt_kind, b))
            if visible_keys(cur_kind, b):
                accumulate(vt, 1 - slot, b, visible_keys(cur_kind, b))
            put_scores(scores, slot, nxt_kind, b, visible_keys(nxt_kind, b))

    for b in all_q:
        if visible_keys(2, b):
            put_scores(qk(2 * qi + 1, b, visible_keys(2, b)), 0, 2, b, visible_keys(2, b))
    stage(2 * qi, 1, 1, 2 * qi + 1, 2)

    @pl.when(qi >= 1)
    def _():
        stage(2 * qi - 1, 0, 0, 2 * qi, 1)
        stage(2 * qi - 2, 1, None, 2 * qi - 1, 0)

    def far_tiles(first, count, in_slot1):
        for t in range(0, count, 2):
            stage(first + t, 0, None, in_slot1, None)
            stage(first + t + 1, 1, None, first + t, None)
            in_slot1 = first + t + 1
        return in_slot1

    n_pairs = jnp.maximum(qi - 1, 0)
    in_slot1 = lax.fori_loop(0, n_pairs // 4, lambda p, in1: far_tiles(8 * p, 8, in1),
                             jnp.maximum(2 * qi - 2, 0))
    for pairs in (2, 1):
        first = 2 * (n_pairs - n_pairs % (2 * pairs))
        pending = n_pairs % (2 * pairs) >= pairs

        @pl.when(pending)
        def _():
            far_tiles(first, 2 * pairs, in_slot1)

        in_slot1 = jnp.where(pending, first + 2 * pairs - 1, in_slot1)

    lam = (jnp.exp(jnp.sum(lq1_ref[...] * lk1_ref[...], axis=-1, keepdims=True))
           - jnp.exp(jnp.sum(lq2_ref[...] * lk2_ref[...], axis=-1, keepdims=True))
           + lambda_init)
    vt = values_t(in_slot1)
    for b in all_q:
        accumulate(vt, 1, b, visible_keys(None, b))
        out = [acc_ref[c, b, :HEAD_DIM, :] / acc_ref[c, b, HEAD_DIM:HEAD_DIM + 1, :] for c in range(2)]
        a = out[0] - lam * out[1]
        a = a * lax.rsqrt(jnp.mean(a * a, axis=0, keepdims=True) + SUBLN_EPS)
        a = a * g_ref[...] * (1.0 - lambda_init)
        o_ref[0, block(b), :] = a.T.astype(o_ref.dtype)


def _attention(qk, vt, bias_tab, lq1, lk1, lq2, lk2, gain, *, lambda_init):
    b, s, _ = qk.shape
    kt = ATTN_K_TILE
    qt = 2 * kt
    nb = qt // QUERY_BLOCK
    vec = lambda n: pl.BlockSpec((1, n), lambda bi, h, qi: (0, 0))
    return pl.pallas_call(
        functools.partial(_attn_kernel, kt=kt, lambda_init=lambda_init),
        grid=(b, N_HEADS, s // qt),
        in_specs=[
            pl.BlockSpec((1, qt, HEAD_DIM), lambda bi, h, qi: (bi, qi, h)),
            pl.BlockSpec((1, s, HEAD_DIM), lambda bi, h, qi: (bi, 0, N_HEADS + h)),
            pl.BlockSpec((HEAD_DIM, s), lambda bi, h, qi: (h, bi)),
            pl.BlockSpec((1, 3, nb, kt, QUERY_BLOCK), lambda bi, h, qi: (h, 0, 0, 0, 0)),
            vec(QK_DIM), vec(QK_DIM), vec(QK_DIM), vec(QK_DIM),
            pl.BlockSpec((HEAD_DIM, 1), lambda bi, h, qi: (0, 0)),
        ],
        out_specs=pl.BlockSpec((1, qt, HEAD_DIM), lambda bi, h, qi: (bi, qi, h)),
        out_shape=jax.ShapeDtypeStruct((b, s, N_HEADS * HEAD_DIM), BF16),
        scratch_shapes=[
            pltpu.VMEM((2, 2, nb, QUERY_BLOCK // LANES, kt, LANES), F32),
            pltpu.VMEM((2, 2, nb, 1, QUERY_BLOCK), F32),
            pltpu.VMEM((2, 2, nb, 1, QUERY_BLOCK), F32),
            pltpu.VMEM((2, nb, HEAD_DIM + SUM_ROWS, QUERY_BLOCK), F32),
        ],
        compiler_params=_params("arbitrary", "arbitrary", "arbitrary"),
        name="diff_attention",
    )(qk, qk, vt, bias_tab, lq1, lk1, lq2, lk2, gain)


def _out_proj_kernel(a_ref, c_ref, w_ref, x_ref, g_ref, o_ref):
    aw = a_ref.shape[1]
    for rows in _row_chunks(a_ref.shape[0], 2 * ROW_CHUNK):
        mix = jnp.dot(a_ref[rows, :], w_ref[:aw, :], preferred_element_type=F32)
        mix = mix + jnp.dot(c_ref[rows, :], w_ref[aw:, :], preferred_element_type=F32)
        o_ref[rows, :] = x_ref[rows, :] + _rms(mix, NORM_EPS) * g_ref[...]


def _out_proj(a, c, w, layer, x, g, *, tm):
    m, d = x.shape
    row = lambda n: pl.BlockSpec((tm, n), lambda i: (i, 0))
    const = lambda shape: pl.BlockSpec(shape, lambda i: (0, 0))
    resident = pl.BlockSpec((None,) + w.shape[1:], lambda i: (layer, 0, 0), pipeline_mode=pl.Buffered(1))
    return pl.pallas_call(
        _out_proj_kernel,
        grid=(m // tm,),
        in_specs=[row(a.shape[1]), row(c.shape[1]), resident, row(d), const((1, d))],
        out_specs=row(d),
        out_shape=jax.ShapeDtypeStruct((m, d), F32),
        compiler_params=_params("arbitrary"),
        name="out_proj",
    )(a, c, w, x, g)


def _ffn_kernel(x_ref, gpre_ref, wgu_ref, wd_ref, gpost_ref, o_ref, hn_ref):
    j = pl.program_id(1)
    last = pl.num_programs(1) - 1
    chunks = _row_chunks(x_ref.shape[0], 2 * ROW_CHUNK)

    def partial_out(hn):
        gu = jnp.dot(hn, wgu_ref[...], preferred_element_type=F32)
        gate, up = gu[:, :gu.shape[1] // 2], gu[:, gu.shape[1] // 2:]
        h = (gate * jax.nn.sigmoid(gate) * up).astype(BF16)
        return jnp.dot(h, wd_ref[...], preferred_element_type=F32)

    @pl.when(j == 0)
    def _():
        for rows in chunks:
            hn = (_rms(x_ref[rows, :], NORM_EPS) * gpre_ref[...]).astype(BF16)
            hn_ref[rows, :] = hn
            o_ref[rows, :] = partial_out(hn)

    @pl.when((j > 0) & (j < last))
    def _():
        o_ref[...] += partial_out(hn_ref[...])

    @pl.when(j == last)
    def _():
        for rows in chunks:
            f = o_ref[rows, :] + partial_out(hn_ref[rows, :])
            o_ref[rows, :] = x_ref[rows, :] + _rms(f, NORM_EPS) * gpost_ref[...]


def _ffn(x, gpre, wgu, wd, layer, gpost, *, tm, tf):
    m, d = x.shape
    f = wd.shape[1]
    assert f // tf >= 2
    return pl.pallas_call(
        _ffn_kernel,
        grid=(m // tm, f // tf),
        in_specs=[
            pl.BlockSpec((tm, d), lambda i, j: (i, 0)),
            pl.BlockSpec((1, d), lambda i, j: (0, 0)),
            pl.BlockSpec((None, d, 2 * tf), lambda i, j: (layer, 0, j)),
            pl.BlockSpec((None, tf, d), lambda i, j: (layer, j, 0)),
            pl.BlockSpec((1, d), lambda i, j: (0, 0)),
        ],
        out_specs=pl.BlockSpec((tm, d), lambda i, j: (i, 0)),
        out_shape=jax.ShapeDtypeStruct((m, d), F32),
        scratch_shapes=[pltpu.VMEM((tm, d), BF16)],
        compiler_params=_params("arbitrary", "arbitrary"),
        name="swiglu_ffn",
    )(x, gpre, wgu, wd, gpost)


def kernel(x, w_in, w_out, lambda_q1, lambda_k1, lambda_q2, lambda_k2, subln_gain, conv_w, conv_norm_gain, rel_bias, w_gate, w_up, w_down, norm_mix_pre, norm_mix_post, norm_ffn_pre, norm_ffn_post):
    b, s, d = x.shape
    depth = w_in.shape[0]
    assert s % (2 * ATTN_K_TILE) == 0 and ATTN_K_TILE >= MAX_DISTANCE
    assert s % ROW_TILE == 0 and w_gate.shape[2] % FFN_CHUNK == 0
    assert w_in.shape[2] == 6 * N_HEADS * HEAD_DIM and 2 * QK_DIM == HEAD_DIM

    w_in, w_out, w_gate, w_up, w_down = (w.astype(BF16) for w in (w_in, w_out, w_gate, w_up, w_down))
    bias_tab = _bias_tiles(rel_bias, ATTN_K_TILE)
    chunked = lambda w: w.reshape(depth, d, -1, FFN_CHUNK)
    w_gu = jnp.concatenate([chunked(w_gate), chunked(w_up)], axis=3).reshape(depth, d, -1)
    row = lambda v: v.reshape(1, -1)

    xf = x.reshape(b * s, d)
    for l in range(depth):
        lambda_init = 0.8 - 0.6 * math.exp(-0.3 * l)
        qk, vt, c = _in_proj(xf, row(norm_mix_pre[l]), w_in, l, conv_w[l], row(conv_norm_gain[l]),
                             tm=ROW_TILE, seq_len=s)
        a = _attention(qk.reshape(b, s, -1), vt, bias_tab, row(lambda_q1[l]), row(lambda_k1[l]),
                       row(lambda_q2[l]), row(lambda_k2[l]), subln_gain[l].reshape(-1, 1),
                       lambda_init=lambda_init)
        xf = _out_proj(a.reshape(b * s, -1), c, w_out, l, xf, row(norm_mix_post[l]), tm=ROW_TILE)
        xf = _ffn(xf, row(norm_ffn_pre[l]), w_gu, w_down, l, row(norm_ffn_post[l]),
                  tm=ROW_TILE, tf=FFN_CHUNK)
    return xf.reshape(b, s, d)
```
